```python
import math
import jax, jax.numpy as jnp
from jax import lax
import numpy as np

D_MODEL = 1024
BATCH = 4
SEQ = 4096
DEPTH = 4
DEC_BATCH = 32
DEC_SEQ = 1
PAST_LEN = 8192
PAGE_SIZE = 128

N_BR = 4
W_BR = D_MODEL // 2
CG_A = 16
G_A = W_BR // CG_A
P_A = 64
CHUNK = 128
HG_B = 4
CW_B = W_BR // HG_B
H_C = 8
HD_C = W_BR // H_C
QBLK = 128
SB_BIAS_INIT = -7.0
H_D = 8
N_D = W_BR // H_D
LR_W = 64
LR_A = 64
SHIFT_W = 3 * W_BR + LR_W + LR_A
GN_EPS = 64e-5
RMS_EPS = 1e-6
LN_EPS = 1e-5

IN_SPLITS = [W_BR, W_BR,
             W_BR, W_BR, W_BR,
             W_BR, W_BR, W_BR, W_BR,
             SHIFT_W, W_BR,
             N_BR * D_MODEL]
IN_W = sum(IN_SPLITS)

kernel_name = "hybrid_s5_gmlp_stickbreak_rwkv7_step"


def rmsnorm(x, g):
    xf = x.astype(jnp.float32)
    y = xf * lax.rsqrt(jnp.mean(xf * xf, axis=-1, keepdims=True) + RMS_EPS)
    return (y * g.astype(jnp.float32)).astype(x.dtype)


def _cplx_combine(e1, e2):
    a1r, a1i, b1r, b1i = e1
    a2r, a2i, b2r, b2i = e2
    return (a2r * a1r - a2i * a1i,
            a2r * a1i + a2i * a1r,
            a2r * b1r - a2i * b1i + b2r,
            a2r * b1i + a2i * b1r + b2i)


def s5_branch(u, h0_re, h0_im, a_re, a_im, log_dt, b_re, b_im, c_re, c_im, d, w_glu, b_glu):
    f32 = jnp.float32
    bsz, t_len, _ = u.shape
    uf = u.astype(f32)
    ug = uf.reshape(bsz, t_len, G_A, CG_A)
    ar, ai = a_re.astype(f32), a_im.astype(f32)
    dt = jnp.exp(log_dt.astype(f32))[:, None]
    mag = jnp.exp(ar * dt)
    abar_re, abar_im = mag * jnp.cos(ai * dt), mag * jnp.sin(ai * dt)
    den = ar * ar + ai * ai
    xr, xi = abar_re - 1.0, abar_im
    coef_re = (xr * ar + xi * ai) / den
    coef_im = (xi * ar - xr * ai) / den
    br, bi = b_re.astype(f32), b_im.astype(f32)
    bbar_re = coef_re[..., None] * br - coef_im[..., None] * bi
    bbar_im = coef_re[..., None] * bi + coef_im[..., None] * br
    bu_re = jnp.einsum('btgc,gpc->btgp', ug, bbar_re)
    bu_im = jnp.einsum('btgc,gpc->btgp', ug, bbar_im)
    shp = bu_re.shape
    acum_re, acum_im, bcum_re, bcum_im = lax.associative_scan(
        _cplx_combine,
        (jnp.broadcast_to(abar_re, shp), jnp.broadcast_to(abar_im, shp), bu_re, bu_im),
        axis=1)
    h0r = h0_re.astype(f32)[:, None]
    h0i = h0_im.astype(f32)[:, None]
    h_re = bcum_re + acum_re * h0r - acum_im * h0i
    h_im = bcum_im + acum_re * h0i + acum_im * h0r
    y = (jnp.einsum('btgp,gcp->btgc', h_re, c_re.astype(f32))
         - jnp.einsum('btgp,gcp->btgc', h_im, c_im.astype(f32)))
    y = y.reshape(bsz, t_len, W_BR) + d.astype(f32) * uf
    y = y * jax.nn.sigmoid(jax.nn.gelu(y) @ w_glu.astype(f32) + b_glu.astype(f32))
    return y, h_re[:, -1], h_im[:, -1]


def sgu_branch(u, v, ln_g, ln_b, w_s, b_s):
    f32 = jnp.float32
    bsz, t_len, _ = u.shape
    vf = v.astype(f32)
    mu = jnp.mean(vf, axis=-1, keepdims=True)
    var = jnp.mean(jnp.square(vf - mu), axis=-1, keepdims=True)
    vn = (vf - mu) * lax.rsqrt(var + LN_EPS) * ln_g.astype(f32) + ln_b.astype(f32)
    n_ch = -(-t_len // CHUNK)
    pad = n_ch * CHUNK - t_len
    vp = jnp.pad(vn, ((0, 0), (0, pad), (0, 0))).reshape(bsz, n_ch, CHUNK, HG_B, CW_B)
    causal = jnp.tril(jnp.ones((CHUNK, CHUNK), dtype=bool))
    ws = jnp.where(causal[None], w_s.astype(f32), 0.0)
    mix = jnp.einsum('hts,bnshc->bnthc', ws, vp) + b_s.astype(f32).T[None, None, :, :, None]
    mix = mix.reshape(bsz, n_ch * CHUNK, W_BR)[:, :t_len]
    return u.astype(f32) * mix, vn


def sb_block(q, k, v, qpos, kpos, bias):
    f32 = jnp.float32
    z = (jnp.einsum('bqhd,bkhd->bhqk', q.astype(f32), k.astype(f32)) * (HD_C ** -0.5)
         + bias.astype(f32)[None, :, None, None])
    mask = kpos[None, :] < qpos[:, None]
    log_beta = jax.nn.log_sigmoid(z)
    l1m = jnp.where(mask, log_beta - z, 0.0)
    suffix = lax.cumsum(l1m, axis=3, reverse=True) - l1m
    att = jnp.where(mask, jnp.exp(log_beta + suffix), 0.0)
    return jnp.einsum('bhqk,bkhd->bqhd', att, v.astype(f32))


def sb_attention(q, k, v, pos0, bias):
    bsz, t_len = q.shape[:2]
    kpos = jnp.arange(k.shape[1])
    qpos = pos0 + jnp.arange(t_len)
    blk = min(QBLK, t_len)
    nb = -(-t_len // blk)
    pad = nb * blk - t_len
    qb = jnp.pad(q, ((0, 0), (0, pad), (0, 0), (0, 0))).reshape(bsz, nb, blk, H_C, HD_C).swapaxes(0, 1)
    qposb = jnp.pad(qpos, (0, pad)).reshape(nb, blk)
    out = lax.map(lambda a: sb_block(a[0], k, v, a[1], kpos, bias), (qb, qposb))
    return out.swapaxes(0, 1).reshape(bsz, nb * blk, H_C, HD_C)[:, :t_len]


def rwkv_branch(cols, shift0, wkv0, mu, w0, w2, a0, a2, k_k, k_a, r_k, lnx_g, lnx_b):
    f32 = jnp.float32
    bsz, t_len, _ = cols.shape
    cf = cols.astype(f32)
    prev = jnp.concatenate([shift0.astype(f32)[:, None], cf[:, :-1]], axis=1)
    xs = cf + (prev - cf) * mu.astype(f32)
    r, k, v, wl, al = jnp.split(xs, np.cumsum([W_BR, W_BR, W_BR, LR_W]).tolist(), axis=-1)
    w = -jax.nn.softplus(-(w0.astype(f32) + jnp.tanh(wl) @ w2.astype(f32))) - 0.5
    a = jax.nn.sigmoid(a0.astype(f32) + al @ a2.astype(f32))
    hs = lambda t: t.reshape(bsz, t_len, H_D, N_D)
    kk = hs(k * k_k.astype(f32))
    kk = kk * lax.rsqrt(jnp.sum(kk * kk, axis=-1, keepdims=True) + 1e-12)
    k = hs(k * (1.0 + (a - 1.0) * k_a.astype(f32)))
    r, v, a, decay = hs(r), hs(v), hs(a), hs(jnp.exp(-jnp.exp(w)))

    def step(S, inp):
        r_t, k_t, v_t, dec_t, kk_t, a_t = inp
        sa = jnp.einsum('bhij,bhj->bhi', S, -kk_t)
        S = (S * dec_t[:, :, None, :] + sa[..., None] * (kk_t * a_t)[:, :, None, :]
             + v_t[..., None] * k_t[:, :, None, :])
        return S, jnp.einsum('bhij,bhj->bhi', S, r_t)

    tm = lambda t: jnp.moveaxis(t, 1, 0)
    wkvT, y = lax.scan(step, wkv0.astype(f32), (tm(r), tm(k), tm(v), tm(decay), tm(kk), tm(a)))
    y = jnp.moveaxis(y, 0, 1)
    ym = jnp.mean(y, axis=-1, keepdims=True)
    yv = jnp.mean(jnp.square(y - ym), axis=-1, keepdims=True)
    y = ((y - ym) * lax.rsqrt(yv + GN_EPS)).reshape(bsz, t_len, W_BR) * lnx_g.astype(f32) + lnx_b.astype(f32)
    y = y + (jnp.sum(r * k * r_k.astype(f32), axis=-1, keepdims=True) * v).reshape(bsz, t_len, W_BR)
    return y, wkvT, cols[:, -1]


def trunk_layer(x, c, lp, h0_re, h0_im, wkv0, shift0, k_past, v_past, pos0):
    bsz, t_len, _ = x.shape
    mod = jax.nn.silu(c) @ lp['w_ada'] + lp['b_ada']
    sh, sc, gt = jnp.split(mod[:, None, :], 3, axis=-1)
    h = rmsnorm(x, lp['norm_g']) * (1.0 + sc) + sh
    z = h @ lp['w_in']
    (a_u, a_g, b_u, b_v, b_g, c_q, c_k, c_v, c_g, d_cols, d_g, m_g) = jnp.split(
        z, np.cumsum(IN_SPLITS)[:-1].tolist(), axis=-1)
    y_a, hT_re, hT_im = s5_branch(a_u, h0_re, h0_im, lp['ssm_a_re'], lp['ssm_a_im'], lp['ssm_log_dt'],
                                  lp['ssm_b_re'], lp['ssm_b_im'], lp['ssm_c_re'], lp['ssm_c_im'],
                                  lp['ssm_d'], lp['ssm_w_glu'], lp['ssm_b_glu'])
    y_b, v_rows = sgu_branch(jax.nn.gelu(b_u), jax.nn.gelu(b_v), lp['sgu_ln_g'], lp['sgu_ln_b'],
                             lp['sgu_w'], lp['sgu_b'])
    heads = lambda t: t.reshape(bsz, t_len, H_C, HD_C)
    q, k_new, v_new = heads(c_q), heads(c_k), heads(c_v)
    if k_past is None:
        k_all, v_all = k_new, v_new
    else:
        k_all = jnp.concatenate([k_past.astype(k_new.dtype), k_new], axis=1)
        v_all = jnp.concatenate([v_past.astype(v_new.dtype), v_new], axis=1)
    y_c = sb_attention(q, k_all, v_all, pos0, lp['sb_bias']).reshape(bsz, t_len, W_BR)
    y_d, wkvT, shiftT = rwkv_branch(d_cols, shift0, wkv0, lp['rwkv_mu'], lp['rwkv_w0'], lp['rwkv_w2'],
                                    lp['rwkv_a0'], lp['rwkv_a2'], lp['rwkv_k_k'], lp['rwkv_k_a'],
                                    lp['rwkv_r_k'], lp['rwkv_lnx_g'], lp['rwkv_lnx_b'])
    ys = jnp.stack([y_a * jax.nn.silu(a_g), y_b * jax.nn.silu(b_g),
                    y_c * jax.nn.silu(c_g), y_d * jax.nn.silu(d_g)], axis=2)
    proj = jnp.einsum('btnw,nwd->btnd', ys, lp['w_branch'])
    merged = jnp.sum(jax.nn.sigmoid(m_g.reshape(bsz, t_len, N_BR, D_MODEL)) * proj, axis=2)
    x_new = (x + gt * (merged @ lp['w_out'])).astype(x.dtype)
    return x_new, (k_new, v_new, hT_re, hT_im, wkvT, shiftT, v_rows)


def setup_inputs(seed: int = 0) -> dict:
    key = jax.random.key(seed)
    ks = iter(jax.random.split(key, 64))
    f32 = jnp.float32
    nrm = lambda shape, s: s * jax.random.normal(next(ks), shape, f32)
    n_pages = PAST_LEN // PAGE_SIZE
    n_pool = (DEC_BATCH * n_pages * 5) // 4
    d_s = D_MODEL ** -0.5
    inp = {}
    inp['x_prompt'] = nrm((BATCH, SEQ, D_MODEL), 1.0)
    inp['x_sample'] = nrm((DEC_BATCH, DEC_SEQ, D_MODEL), 1.0)
    inp['cache_k'] = nrm((DEPTH, n_pool, PAGE_SIZE, H_C, HD_C), 1.0)
    inp['cache_v'] = nrm((DEPTH, n_pool, PAGE_SIZE, H_C, HD_C), 1.0)
    inp['state_ssm_re'] = nrm((DEPTH, DEC_BATCH, G_A, P_A), 0.5)
    inp['state_ssm_im'] = nrm((DEPTH, DEC_BATCH, G_A, P_A), 0.5)
    inp['state_wkv'] = nrm((DEPTH, DEC_BATCH, H_D, N_D, N_D), 0.5)
    inp['state_shift'] = nrm((DEPTH, DEC_BATCH, SHIFT_W), 1.0)
    inp['page_table'] = jax.random.permutation(next(ks), n_pool)[:DEC_BATCH * n_pages].reshape(
        DEC_BATCH, n_pages).astype(jnp.int32)
    inp['c_prompt'] = nrm((BATCH, D_MODEL), 1.0)
    inp['c_sample'] = nrm((DEC_BATCH, D_MODEL), 1.0)
    inp['norm_g'] = 1.0 + nrm((DEPTH, D_MODEL), 0.05)
    inp['w_ada'] = nrm((DEPTH, D_MODEL, 3 * D_MODEL), 0.5 * d_s)
    inp['b_ada'] = nrm((DEPTH, 3 * D_MODEL), 0.02)
    inp['w_in'] = nrm((DEPTH, D_MODEL, IN_W), d_s)
    inp['ssm_a_re'] = -0.5 + nrm((DEPTH, G_A, P_A), 0.02)
    inp['ssm_a_im'] = jnp.pi * jnp.arange(P_A, dtype=f32) + nrm((DEPTH, G_A, P_A), 0.02)
    inp['ssm_log_dt'] = jax.random.uniform(next(ks), (DEPTH, G_A), f32,
                                           minval=math.log(1e-3), maxval=math.log(1e-1))
    inp['ssm_b_re'] = nrm((DEPTH, G_A, P_A, CG_A), (2 * CG_A) ** -0.5)
    inp['ssm_b_im'] = nrm((DEPTH, G_A, P_A, CG_A), (2 * CG_A) ** -0.5)
    inp['ssm_c_re'] = nrm((DEPTH, G_A, CG_A, P_A), (2 * P_A) ** -0.5)
    inp['ssm_c_im'] = nrm((DEPTH, G_A, CG_A, P_A), (2 * P_A) ** -0.5)
    inp['ssm_d'] = nrm((DEPTH, W_BR), 1.0)
    inp['ssm_w_glu'] = nrm((DEPTH, W_BR, W_BR), W_BR ** -0.5)
    inp['ssm_b_glu'] = nrm((DEPTH, W_BR), 0.02)
    inp['sgu_ln_g'] = 1.0 + nrm((DEPTH, W_BR), 0.05)
    inp['sgu_ln_b'] = nrm((DEPTH, W_BR), 0.02)
    inp['sgu_w'] = nrm((DEPTH, HG_B, CHUNK, CHUNK), CHUNK ** -0.5)
    inp['sgu_b'] = 1.0 + nrm((DEPTH, HG_B, CHUNK), 0.1)
    inp['sb_bias'] = SB_BIAS_INIT + nrm((DEPTH, H_C), 0.5)
    inp['rwkv_mu'] = jax.random.uniform(next(ks), (DEPTH, SHIFT_W), f32)
    inp['rwkv_w0'] = -1.0 + nrm((DEPTH, W_BR), 0.5)
    inp['rwkv_w2'] = nrm((DEPTH, LR_W, W_BR), 0.1)
    inp['rwkv_a0'] = nrm((DEPTH, W_BR), 0.5)
    inp['rwkv_a2'] = nrm((DEPTH, LR_A, W_BR), 0.5 * LR_A ** -0.5)
    inp['rwkv_k_k'] = 1.0 + nrm((DEPTH, W_BR), 0.1)
    inp['rwkv_k_a'] = 1.0 + nrm((DEPTH, W_BR), 0.1)
    inp['rwkv_r_k'] = nrm((DEPTH, H_D, N_D), 0.1)
    inp['rwkv_lnx_g'] = 1.0 + nrm((DEPTH, W_BR), 0.05)
    inp['rwkv_lnx_b'] = nrm((DEPTH, W_BR), 0.02)
    inp['w_branch'] = nrm((DEPTH, N_BR, W_BR, D_MODEL), W_BR ** -0.5)
    inp['w_out'] = nrm((DEPTH, D_MODEL, D_MODEL), d_s)
    inp['final_norm_g'] = 1.0 + nrm((D_MODEL,), 0.05)
    return inp


def reference(x_prompt, x_sample, cache_k, cache_v, state_ssm_re, state_ssm_im, state_wkv, state_shift,
              page_table, c_prompt, c_sample, norm_g, w_ada, b_ada, w_in,
              ssm_a_re, ssm_a_im, ssm_log_dt, ssm_b_re, ssm_b_im, ssm_c_re, ssm_c_im, ssm_d,
              ssm_w_glu, ssm_b_glu, sgu_ln_g, sgu_ln_b, sgu_w, sgu_b, sb_bias,
              rwkv_mu, rwkv_w0, rwkv_w2, rwkv_a0, rwkv_a2, rwkv_k_k, rwkv_k_a, rwkv_r_k,
              rwkv_lnx_g, rwkv_lnx_b, w_branch, w_out, final_norm_g):
    f32 = jnp.float32
    bp = x_prompt.shape[0]
    db = x_sample.shape[0]
    n_pages = page_table.shape[1]
    xp, xs = x_prompt, x_sample
    outs_p, outs_s = [], []
    for l in range(DEPTH):
        lp = dict(norm_g=norm_g[l], w_ada=w_ada[l], b_ada=b_ada[l], w_in=w_in[l],
                  ssm_a_re=ssm_a_re[l], ssm_a_im=ssm_a_im[l], ssm_log_dt=ssm_log_dt[l],
                  ssm_b_re=ssm_b_re[l], ssm_b_im=ssm_b_im[l], ssm_c_re=ssm_c_re[l], ssm_c_im=ssm_c_im[l],
                  ssm_d=ssm_d[l], ssm_w_glu=ssm_w_glu[l], ssm_b_glu=ssm_b_glu[l],
                  sgu_ln_g=sgu_ln_g[l], sgu_ln_b=sgu_ln_b[l], sgu_w=sgu_w[l], sgu_b=sgu_b[l],
                  sb_bias=sb_bias[l],
                  rwkv_mu=rwkv_mu[l], rwkv_w0=rwkv_w0[l], rwkv_w2=rwkv_w2[l], rwkv_a0=rwkv_a0[l],
                  rwkv_a2=rwkv_a2[l], rwkv_k_k=rwkv_k_k[l], rwkv_k_a=rwkv_k_a[l], rwkv_r_k=rwkv_r_k[l],
                  rwkv_lnx_g=rwkv_lnx_g[l], rwkv_lnx_b=rwkv_lnx_b[l],
                  w_branch=w_branch[l], w_out=w_out[l])
        zh = jnp.zeros((bp, G_A, P_A), f32)
        xp, st_p = trunk_layer(xp, c_prompt, lp, zh, zh, jnp.zeros((bp, H_D, N_D, N_D), f32),
                               jnp.zeros((bp, SHIFT_W), f32), None, None, 0)
        k_past = cache_k[l][page_table].reshape(db, n_pages * PAGE_SIZE, H_C, HD_C)
        v_past = cache_v[l][page_table].reshape(db, n_pages * PAGE_SIZE, H_C, HD_C)
        xs, st_s = trunk_layer(xs, c_sample, lp, state_ssm_re[l], state_ssm_im[l], state_wkv[l],
                               state_shift[l], k_past, v_past, PAST_LEN)
        outs_p.append(st_p)
        outs_s.append(st_s)
    stk = lambda outs, i, dt: jnp.stack([o[i] for o in outs]).astype(dt)
    y_prompt = rmsnorm(xp, final_norm_g)
    y_sample = rmsnorm(xs, final_norm_g)
    k_prompt = stk(outs_p, 0, cache_k.dtype)
    v_prompt = stk(outs_p, 1, cache_v.dtype)
    k_sample = stk(outs_s, 0, cache_k.dtype)
    v_sample = stk(outs_s, 1, cache_v.dtype)
    ssm_re_prompt = stk(outs_p, 2, state_ssm_re.dtype)
    ssm_im_prompt = stk(outs_p, 3, state_ssm_im.dtype)
    ssm_re_sample = stk(outs_s, 2, state_ssm_re.dtype)
    ssm_im_sample = stk(outs_s, 3, state_ssm_im.dtype)
    wkv_prompt = stk(outs_p, 4, state_wkv.dtype)
    wkv_sample = stk(outs_s, 4, state_wkv.dtype)
    shift_prompt = stk(outs_p, 5, state_shift.dtype)
    shift_sample = stk(outs_s, 5, state_shift.dtype)
    sgu_v_sample = stk(outs_s, 6, x_sample.dtype)
    return (y_prompt, y_sample, k_prompt, v_prompt, k_sample, v_sample,
            ssm_re_prompt, ssm_im_prompt, ssm_re_sample, ssm_im_sample,
            wkv_prompt, wkv_sample, shift_prompt, shift_sample, sgu_v_sample)
```

```python
import functools
import math

import jax
import jax.numpy as jnp
import numpy as np
from jax import lax
from jax.experimental import pallas as pl
from jax.experimental.pallas import tpu as pltpu

F32 = jnp.float32
BF16 = jnp.bfloat16

D_MODEL = 1024
W_BR = D_MODEL // 2
N_BR = 4
CG_A = 16
G_A = W_BR // CG_A
P_A = 64
S5_W = G_A * P_A
CHUNK = 128
HG_B = 4
H_C = 8
HD_C = W_BR // H_C
H_D = 8
N_D = W_BR // H_D
LR_W = 64
LR_A = 64
SHIFT_W = 3 * W_BR + LR_W + LR_A
GN_EPS = 64e-5
RMS_EPS = 1e-6
LN_EPS = 1e-5
LANE = 128
SUBLANE = 8

COL_A_U, COL_A_G = 0, 512
COL_B_U, COL_B_V, COL_B_G = 1024, 1536, 2048
COL_C_Q, COL_C_K, COL_C_V, COL_C_G = 2560, 3072, 3584, 4096
COL_D_R, COL_D_K, COL_D_V, COL_D_WA = 4608, 5120, 5632, 6144
COL_D_G = 6656
COL_M_G = 7168
IN_W_PAD = COL_M_G + N_BR * D_MODEL
IN_W_SRC_D_G = COL_D_WA + LR_W + LR_A

VMEM_LIMIT = 56 * 2**20


def _params(sem, vmem=VMEM_LIMIT):
    return pltpu.CompilerParams(dimension_semantics=sem, vmem_limit_bytes=vmem)


def _silu(x):
    return x * jax.nn.sigmoid(x)


def _gelu(x):
    c = math.sqrt(2.0 / math.pi)
    return 0.5 * x * (1.0 + jnp.tanh(c * (x + 0.044715 * (x * x * x))))


def _softplus(x):
    return jnp.maximum(x, 0.0) + jnp.log(1.0 + jnp.exp(-jnp.abs(x)))


def _bdot(a, b):
    return jnp.dot(a.astype(BF16), b.astype(BF16), preferred_element_type=F32)


def _split_dot(x, w, passes):
    acc = None
    rem = x
    for p in range(passes):
        piece = rem.astype(BF16)
        term = jnp.dot(piece, w, preferred_element_type=F32)
        acc = term if acc is None else acc + term
        if p + 1 < passes:
            rem = rem - piece.astype(F32)
    return acc


def _ada_kernel(c_ref, w_ref, b_ref, o_ref):
    o_ref[...] = _bdot(_silu(c_ref[...]), w_ref[...]) + b_ref[...]


def _ada_mod(c_all, w_ada, b_ada):
    depth = w_ada.shape[0]
    rows = c_all.shape[0]
    return pl.pallas_call(
        _ada_kernel,
        grid=(depth, 3),
        in_specs=[
            pl.BlockSpec((rows, D_MODEL), lambda l, k: (0, 0)),
            pl.BlockSpec((None, D_MODEL, D_MODEL), lambda l, k: (l, 0, k)),
            pl.BlockSpec((None, 1, D_MODEL), lambda l, k: (l, 0, k)),
        ],
        out_specs=pl.BlockSpec((None, rows, D_MODEL), lambda l, k: (l, 0, k)),
        out_shape=jax.ShapeDtypeStruct((depth, rows, 3 * D_MODEL), F32),
        compiler_params=_params(("arbitrary", "arbitrary")),
        name="ada_mod",
    )(c_all, w_ada, b_ada.reshape(depth, 1, 3 * D_MODEL))


def _inproj_kernel(x_ref, g_ref, sh_ref, sc_ref, w_ref, z_ref, h_scr):
    @pl.when(pl.program_id(1) == 0)
    def _():
        x = x_ref[...]
        ms = jnp.mean(x * x, axis=-1, keepdims=True)
        y = x * lax.rsqrt(ms + RMS_EPS) * g_ref[...]
        h_scr[...] = (y * (1.0 + sc_ref[...]) + sh_ref[...]).astype(BF16)

    z_ref[...] = jnp.dot(h_scr[...], w_ref[...], preferred_element_type=F32)


def _in_proj(x, mod, norm_g, w_in, rows_per_group):
    n = x.shape[0]
    tm = min(1024, rows_per_group)
    tn = 1024
    r = mod.shape[1]
    bpg = rows_per_group // tm
    mod_spec = lambda k: pl.BlockSpec((None, r, D_MODEL), lambda i, j: (i // bpg, 0, k))
    return pl.pallas_call(
        _inproj_kernel,
        grid=(n // tm, IN_W_PAD // tn),
        in_specs=[
            pl.BlockSpec((tm, D_MODEL), lambda i, j: (i, 0)),
            pl.BlockSpec((1, D_MODEL), lambda i, j: (0, 0)),
            mod_spec(0),
            mod_spec(1),
            pl.BlockSpec((D_MODEL, tn), lambda i, j: (0, j)),
        ],
        out_specs=pl.BlockSpec((tm, tn), lambda i, j: (i, j)),
        out_shape=jax.ShapeDtypeStruct((n, IN_W_PAD), F32),
        scratch_shapes=[pltpu.VMEM((tm, D_MODEL), BF16)],
        compiler_params=_params(("arbitrary", "arbitrary")),
        name="in_proj",
    )(x, norm_g.reshape(1, D_MODEL), mod, mod, w_in)


def _s5_disc_kernel(ar_ref, ai_ref, ldt_ref, br_ref, bi_ref,
                    abr_ref, abi_ref, bbr_ref, bbi_ref):
    ar, ai = ar_ref[...], ai_ref[...]
    dt = jnp.exp(ldt_ref[...])
    mag = jnp.exp(ar * dt)
    abar_re, abar_im = mag * jnp.cos(ai * dt), mag * jnp.sin(ai * dt)
    den = ar * ar + ai * ai
    xr, xi = abar_re - 1.0, abar_im
    coef_re = (xr * ar + xi * ai) / den
    coef_im = (xi * ar - xr * ai) / den
    br, bi = br_ref[...], bi_ref[...]
    abr_ref[...] = abar_re
    abi_ref[...] = abar_im
    bbr_ref[...] = coef_re * br - coef_im * bi
    bbi_ref[...] = coef_re * bi + coef_im * br


def _s5_discretise(a_re, a_im, log_dt, b_re, b_im):
    gp = jax.ShapeDtypeStruct((G_A, 1, P_A), F32)
    gcp = jax.ShapeDtypeStruct((G_A, CG_A, P_A), F32)
    return pl.pallas_call(
        _s5_disc_kernel,
        out_shape=(gp, gp, gcp, gcp),
        name="s5_discretise",
    )(a_re.reshape(G_A, 1, P_A), a_im.reshape(G_A, 1, P_A), log_dt.reshape(G_A, 1, 1),
      jnp.swapaxes(b_re, 1, 2), jnp.swapaxes(b_im, 1, 2))


def _s5_dense_weights(abar_re, abar_im, bbar_re, bbar_im, c_re, c_im, nb):
    eye = jnp.eye(G_A, dtype=F32)
    bd = lambda m: jnp.einsum('gcp,gh->gchp', m, eye).reshape(W_BR, S5_W).astype(BF16)
    cd = lambda m: jnp.einsum('gcp,gh->gphc', m, eye).reshape(S5_W, W_BR).astype(BF16)
    ar = abar_re.reshape(1, S5_W)
    ai = abar_im.reshape(1, S5_W)
    a1 = jnp.broadcast_to(ar, (2 * nb, S5_W))
    a2 = jnp.concatenate([jnp.broadcast_to(-ai, (nb, S5_W)), jnp.broadcast_to(ai, (nb, S5_W))], axis=0)
    return bd(bbar_re), bd(bbar_im), cd(c_re), cd(c_im), a1, a2


def _s5_tail(y, u, g, d_ref, wglu_ref, bglu_ref):
    y = y + d_ref[...] * u
    y = y * jax.nn.sigmoid(_bdot(_gelu(y), wglu_ref[...]) + bglu_ref[...])
    return y * _silu(g)


def _s5_prompt_kernel(u_ref, g_ref, bdr_ref, bdi_ref, cdr_ref, cdi_ref, a1_ref, a2_ref,
                      d_ref, wglu_ref, bglu_ref, y_ref, hT_ref, hs_scr, h_scr, *, nb, lc):
    c = pl.program_id(0)

    @pl.when(c == 0)
    def _():
        h_scr[...] = jnp.zeros_like(h_scr)

    rows = nb * lc
    n_cg = S5_W // LANE
    u = u_ref[...].reshape(rows, W_BR)
    ub = u.astype(BF16)
    for half, bd_ref in enumerate((bdr_ref, bdi_ref)):
        bu = jnp.dot(ub, bd_ref[...], preferred_element_type=F32)
        for cg in range(n_cg):
            hs_scr[cg, half * rows:(half + 1) * rows, :] = bu[:, cg * LANE:(cg + 1) * LANE]
    a1 = [a1_ref[cg] for cg in range(n_cg)]
    a2 = [a2_ref[cg] for cg in range(n_cg)]

    def step(t, hs):
        idx = pl.ds(t, 2 * nb, stride=lc)
        out = []
        for cg in range(n_cg):
            h = a1[cg] * hs[cg] + a2[cg] * pltpu.roll(hs[cg], nb, 0) + hs_scr[cg, idx, :]
            hs_scr[cg, idx, :] = h
            out.append(h)
        return tuple(out)

    hs = lax.fori_loop(0, lc, step, tuple(h_scr[cg] for cg in range(n_cg)))
    for cg in range(n_cg):
        h_scr[cg] = hs[cg]
        hT_ref[cg] = hs[cg]
    h_re = jnp.concatenate([hs_scr[cg, 0:rows, :] for cg in range(n_cg)], axis=1)
    h_im = jnp.concatenate([hs_scr[cg, rows:2 * rows, :] for cg in range(n_cg)], axis=1)
    y = _bdot(h_re, cdr_ref[...]) - _bdot(h_im, cdi_ref[...])
    g = g_ref[...].reshape(rows, W_BR)
    out = _s5_tail(y, u, g, d_ref, wglu_ref, bglu_ref)
    y_ref[...] = out.reshape(nb, lc, W_BR).astype(y_ref.dtype)


def _s5_prompt(z3, dense, d, w_glu, b_glu):
    nb, t_len, _ = z3.shape
    assert 2 * nb == SUBLANE, "state rows [re; im] of all sequences must fill one sublane tile"
    lc = min(128, t_len)
    bdr, bdi, cdr, cdi, a1, a2 = dense
    full = lambda shape: pl.BlockSpec(shape, lambda c: (0,) * len(shape))
    kern = functools.partial(_s5_prompt_kernel, nb=nb, lc=lc)
    n_cg = S5_W // LANE
    slabs = lambda a: a.reshape(2 * nb, n_cg, LANE).transpose(1, 0, 2)
    y, hT = pl.pallas_call(
        kern,
        grid=(t_len // lc,),
        in_specs=[
            pl.BlockSpec((nb, lc, W_BR), lambda c: (0, c, COL_A_U // W_BR)),
            pl.BlockSpec((nb, lc, W_BR), lambda c: (0, c, COL_A_G // W_BR)),
            full((W_BR, S5_W)), full((W_BR, S5_W)), full((S5_W, W_BR)), full((S5_W, W_BR)),
            full((n_cg, 2 * nb, LANE)), full((n_cg, 2 * nb, LANE)),
            full((1, W_BR)), full((W_BR, W_BR)), full((1, W_BR)),
        ],
        out_specs=[
            pl.BlockSpec((nb, lc, W_BR), lambda c: (0, c, 0)),
            full((n_cg, 2 * nb, LANE)),
        ],
        out_shape=(jax.ShapeDtypeStruct((nb, t_len, W_BR), BF16),
                   jax.ShapeDtypeStruct((n_cg, 2 * nb, LANE), F32)),
        scratch_shapes=[pltpu.VMEM((n_cg, 2 * nb * lc, LANE), F32), pltpu.VMEM((n_cg, 2 * nb, LANE), F32)],
        compiler_params=_params(("arbitrary",)),
        name="s5_prompt",
    )(z3, z3, bdr, bdi, cdr, cdi, slabs(a1), slabs(a2), d.reshape(1, W_BR), w_glu, b_glu.reshape(1, W_BR))
    return y, hT.transpose(1, 0, 2).reshape(2 * nb, S5_W)


def _s5_sample_kernel(u_ref, g_ref, h0r_ref, h0i_ref, bdr_ref, bdi_ref, cdr_ref, cdi_ref,
                      ar_ref, ai_ref, d_ref, wglu_ref, bglu_ref, y_ref, hr_ref, hi_ref):
    u = u_ref[...]
    ub = u.astype(BF16)
    ar, ai = ar_ref[...], ai_ref[...]
    h0r, h0i = h0r_ref[...], h0i_ref[...]
    hr = jnp.dot(ub, bdr_ref[...], preferred_element_type=F32) + (ar * h0r - ai * h0i)
    hi = jnp.dot(ub, bdi_ref[...], preferred_element_type=F32) + (ar * h0i + ai * h0r)
    hr_ref[...] = hr
    hi_ref[...] = hi
    y = _bdot(hr, cdr_ref[...]) - _bdot(hi, cdi_ref[...])
    y_ref[...] = _s5_tail(y, u, g_ref[...], d_ref, wglu_ref, bglu_ref).astype(y_ref.dtype)


def _s5_sample(z, h0_re, h0_im, dense, abar_re, abar_im, d, w_glu, b_glu):
    nb = z.shape[0]
    bdr, bdi, cdr, cdi, _, _ = dense
    full = lambda shape: pl.BlockSpec(shape, lambda i: (0,) * len(shape))
    st = jax.ShapeDtypeStruct((nb, S5_W), F32)
    return pl.pallas_call(
        _s5_sample_kernel,
        grid=(1,),
        in_specs=[
            pl.BlockSpec((nb, W_BR), lambda i: (0, COL_A_U // W_BR)),
            pl.BlockSpec((nb, W_BR), lambda i: (0, COL_A_G // W_BR)),
            full((nb, S5_W)), full((nb, S5_W)),
            full((W_BR, S5_W)), full((W_BR, S5_W)), full((S5_W, W_BR)), full((S5_W, W_BR)),
            full((1, S5_W)), full((1, S5_W)),
            full((1, W_BR)), full((W_BR, W_BR)), full((1, W_BR)),
        ],
        out_specs=[full((nb, W_BR)), full((nb, S5_W)), full((nb, S5_W))],
        out_shape=(jax.ShapeDtypeStruct((nb, W_BR), BF16), st, st),
        compiler_params=_params(("arbitrary",)),
        name="s5_sample",
    )(z, z, h0_re, h0_im, bdr, bdi, cdr, cdi, abar_re.reshape(1, S5_W), abar_im.reshape(1, S5_W),
      d.reshape(1, W_BR), w_glu, b_glu.reshape(1, W_BR))


def _layernorm(v, g_ref, b_ref):
    mu = jnp.mean(v, axis=-1, keepdims=True)
    vc = v - mu
    var = jnp.mean(vc * vc, axis=-1, keepdims=True)
    return vc * lax.rsqrt(var + LN_EPS) * g_ref[...] + b_ref[...]


def _sgu_prompt_kernel(u_ref, v_ref, g_ref, lng_ref, lnb_ref, ws_ref, bs_ref, y_ref, *, n_chunks):
    cw = W_BR // HG_B
    vn = _layernorm(_gelu(v_ref[...]), lng_ref, lnb_ref)
    row = lax.broadcasted_iota(jnp.int32, (CHUNK, CHUNK), 0)
    col = lax.broadcasted_iota(jnp.int32, (CHUNK, CHUNK), 1)
    ws = [jnp.where(col <= row, ws_ref[h], 0.0).astype(BF16) for h in range(HG_B)]
    bs = bs_ref[...]
    for c in range(n_chunks):
        rows = slice(c * CHUNK, (c + 1) * CHUNK)
        vb = vn[rows, :].astype(BF16)
        mix = jnp.concatenate(
            [jnp.dot(ws[h], vb[:, h * cw:(h + 1) * cw], preferred_element_type=F32) for h in range(HG_B)],
            axis=1) + bs
        out = _gelu(u_ref[rows, :]) * mix * _silu(g_ref[rows, :])
        y_ref[rows, :] = out.astype(y_ref.dtype)


def _sgu_prompt(z, ln_g, ln_b, w_s, b_s):
    n = z.shape[0]
    tm = min(512, n)
    full = lambda shape: pl.BlockSpec(shape, lambda i: (0,) * len(shape))
    bs_full = jnp.repeat(b_s.T, W_BR // HG_B, axis=1)
    kern = functools.partial(_sgu_prompt_kernel, n_chunks=tm // CHUNK)
    col = lambda c: pl.BlockSpec((tm, W_BR), lambda i: (i, c // W_BR))
    return pl.pallas_call(
        kern,
        grid=(n // tm,),
        in_specs=[col(COL_B_U), col(COL_B_V), col(COL_B_G),
                  full((1, W_BR)), full((1, W_BR)), full((HG_B, CHUNK, CHUNK)), full((CHUNK, W_BR))],
        out_specs=pl.BlockSpec((tm, W_BR), lambda i: (i, 0)),
        out_shape=jax.ShapeDtypeStruct((n, W_BR), BF16),
        compiler_params=_params(("arbitrary",)),
        name="sgu_prompt",
    )(z, z, z, ln_g.reshape(1, W_BR), ln_b.reshape(1, W_BR), w_s, bs_full)


def _sgu_sample_kernel(u_ref, v_ref, g_ref, lng_ref, lnb_ref, w00_ref, b0_ref, y_ref, vn_ref):
    vn = _layernorm(_gelu(v_ref[...]), lng_ref, lnb_ref)
    vn_ref[...] = vn
    mix = vn * w00_ref[...] + b0_ref[...]
    y_ref[...] = (_gelu(u_ref[...]) * mix * _silu(g_ref[...])).astype(y_ref.dtype)


def _sgu_sample(z, ln_g, ln_b, w_s, b_s):
    nb = z.shape[0]
    cw = W_BR // HG_B
    full = lambda shape: pl.BlockSpec(shape, lambda i: (0,) * len(shape))
    col = lambda c: pl.BlockSpec((nb, W_BR), lambda i: (0, c // W_BR))
    w00 = jnp.repeat(w_s[:, 0, 0], cw).reshape(1, W_BR)
    b0 = jnp.repeat(b_s[:, 0], cw).reshape(1, W_BR)
    return pl.pallas_call(
        _sgu_sample_kernel,
        grid=(1,),
        in_specs=[col(COL_B_U), col(COL_B_V), col(COL_B_G),
                  full((1, W_BR)), full((1, W_BR)), full((1, W_BR)), full((1, W_BR))],
        out_specs=[full((nb, W_BR)), full((nb, W_BR))],
        out_shape=(jax.ShapeDtypeStruct((nb, W_BR), BF16), jax.ShapeDtypeStruct((nb, W_BR), F32)),
        compiler_params=_params(("arbitrary",)),
        name="sgu_sample",
    )(z, z, z, ln_g.reshape(1, W_BR), ln_b.reshape(1, W_BR), w00, b0)


def _suffix_matrix():
    j = np.arange(LANE)[:, None]
    s = np.arange(LANE)[None, :]
    return jnp.asarray(np.concatenate([(j >= s), np.ones((LANE, LANE), bool)], axis=1), dtype=BF16)


def _sb_prompt_kernel(bias_ref, q_ref, k_ref, v_ref, g_ref, m_ref, o_ref,
                      acc0, acc1, car0, car1, *, tq, tk):
    hp = pl.program_id(1)
    i = pl.program_id(2)
    j = pl.program_id(3)
    nsub = tk // LANE
    accs, cars = (acc0, acc1), (car0, car1)

    @pl.when(j == 0)
    def _():
        for r in (acc0, acc1, car0, car1):
            r[...] = jnp.zeros_like(r)

    def process(masked):
        lane = lax.broadcasted_iota(jnp.int32, (tq, LANE), 1)
        q = q_ref[...] * (HD_C ** -0.5)
        qh = (jnp.where(lane < HD_C, q, 0.0).astype(BF16), jnp.where(lane >= HD_C, q, 0.0).astype(BF16))
        m = m_ref[...]
        for sub in reversed(range(nsub)):
            kb = k_ref[sub * LANE:(sub + 1) * LANE, :].astype(BF16)
            vb = v_ref[sub * LANE:(sub + 1) * LANE, :].astype(BF16)
            if masked:
                qpos = lax.broadcasted_iota(jnp.int32, (tq, LANE), 0)
                valid = (lane + sub * LANE) < qpos
            for h in range(2):
                s = lax.dot_general(qh[h], kb, (((1,), (1,)), ((), ())), preferred_element_type=F32)
                z = s + bias_ref[2 * hp + h]
                sp = _softplus(z)
                if masked:
                    sp = jnp.where(valid, sp, 0.0)
                r = _split_dot(sp, m, 2)
                e = jnp.exp(z - r[:, :LANE] - cars[h][...])
                if masked:
                    e = jnp.where(valid, e, 0.0)
                accs[h][...] += jnp.dot(e.astype(BF16), vb, preferred_element_type=F32)
                cars[h][...] += r[:, LANE:]

    @pl.when(j == 0)
    def _():
        process(True)

    @pl.when(jnp.logical_and(j > 0, j <= i))
    def _():
        process(False)

    @pl.when(j == i)
    def _():
        lane = lax.broadcasted_iota(jnp.int32, (tq, LANE), 1)
        y = jnp.where(lane < HD_C, acc0[...], acc1[...])
        o_ref[...] = (y * _silu(g_ref[...])).astype(o_ref.dtype)


def _sb_prompt(z, sb_bias, nb, t_len):
    tq = tk = min(512, t_len)
    nq = t_len // tq
    kern = functools.partial(_sb_prompt_kernel, tq=tq, tk=tk)
    qspec = lambda c: pl.BlockSpec((tq, LANE), lambda b, h, i, j: (b * nq + i, c // LANE + h))
    kspec = lambda c: pl.BlockSpec((tk, LANE), lambda b, h, i, j: (b * nq + jnp.maximum(i - j, 0), c // LANE + h))
    return pl.pallas_call(
        kern,
        grid=(nb, H_C // 2, nq, nq),
        in_specs=[
            pl.BlockSpec(memory_space=pltpu.SMEM),
            qspec(COL_C_Q), kspec(COL_C_K), kspec(COL_C_V), qspec(COL_C_G),
            pl.BlockSpec((LANE, 2 * LANE), lambda b, h, i, j: (0, 0)),
        ],
        out_specs=pl.BlockSpec((tq, LANE), lambda b, h, i, j: (b * nq + i, h)),
        out_shape=jax.ShapeDtypeStruct((nb * t_len, W_BR), BF16),
        scratch_shapes=[pltpu.VMEM((tq, LANE), F32)] * 4,
        compiler_params=_params(("arbitrary",) * 4),
        name="sb_prompt",
    )(sb_bias, z, z, z, z, _suffix_matrix())


PAGES_PER_STEP = 8


def _sb_sample_kernel(pt_ref, q_ref, g_ref, bias_ref, seg_ref, exp_ref, u_ref, *rest, page):
    k_refs = rest[:PAGES_PER_STEP]
    v_refs = rest[PAGES_PER_STEP:2 * PAGES_PER_STEP]
    o_ref, acc, car = rest[2 * PAGES_PER_STEP:]
    jj = pl.program_id(1)

    @pl.when(jj == 0)
    def _():
        acc[...] = jnp.zeros_like(acc)
        car[...] = jnp.zeros_like(car)

    q = q_ref[...] * (HD_C ** -0.5)
    u = u_ref[...]
    for r in reversed(range(PAGES_PER_STEP)):
        k = k_refs[r][...]
        z = jnp.dot((k * q).astype(BF16), seg_ref[...], preferred_element_type=F32) + bias_ref[...]
        sp = _softplus(z)
        hi = sp.astype(BF16)
        lo = (sp - hi.astype(F32)).astype(BF16)
        suf = jnp.dot(u, hi, preferred_element_type=F32) + jnp.dot(u, lo, preferred_element_type=F32)
        e = jnp.exp(z - suf - car[...])
        w = jnp.dot(e.astype(BF16), exp_ref[...], preferred_element_type=F32)
        prod = w * v_refs[r][...]
        acc[...] += jnp.sum(prod.reshape(page // SUBLANE, SUBLANE, W_BR), axis=0)
        car[...] += suf[0:1, :]

    @pl.when(jj == pl.num_programs(1) - 1)
    def _():
        o_ref[...] = (jnp.sum(acc[...], axis=0, keepdims=True) * _silu(g_ref[...])).astype(o_ref.dtype)


def _sb_sample(z, cache_k, cache_v, page_table, sb_bias, base):
    nb = z.shape[0]
    n_pages = page_table.shape[1]
    page = cache_k.shape[1]
    assert n_pages % PAGES_PER_STEP == 0
    n_steps = n_pages // PAGES_PER_STEP
    seg = np.zeros((W_BR, LANE), np.float32)
    seg[np.arange(W_BR), np.arange(W_BR) // HD_C] = 1.0
    expand = np.zeros((LANE, W_BR), np.float32)
    expand[np.arange(W_BR) // HD_C, np.arange(W_BR)] = 1.0
    upper = (np.arange(page)[None, :] >= np.arange(page)[:, None]).astype(np.float32)
    bias_row = jnp.zeros((1, LANE), F32).at[0, :H_C].set(sb_bias)
    q3 = z[:, COL_C_Q:COL_C_Q + W_BR].reshape(nb, 1, W_BR)
    g3 = z[:, COL_C_G:COL_C_G + W_BR].reshape(nb, 1, W_BR)
    row_spec = pl.BlockSpec((None, 1, W_BR), lambda b, jj, pt: (b, 0, 0))

    def page_spec(r):
        def imap(b, jj, pt):
            return (base + pt[b, (n_steps - 1 - jj) * PAGES_PER_STEP + r], 0, 0)
        return pl.BlockSpec((None, page, W_BR), imap)

    const = lambda shape: pl.BlockSpec(shape, lambda b, jj, pt: (0,) * len(shape))
    kern = functools.partial(_sb_sample_kernel, page=page)
    grid_spec = pltpu.PrefetchScalarGridSpec(
        num_scalar_prefetch=1,
        grid=(nb, n_steps),
        in_specs=[row_spec, row_spec,
                  const((1, LANE)), const((W_BR, LANE)), const((LANE, W_BR)), const((page, page))]
                 + [page_spec(r) for r in range(PAGES_PER_STEP)]
                 + [page_spec(r) for r in range(PAGES_PER_STEP)],
        out_specs=pl.BlockSpec((None, 1, W_BR), lambda b, jj, pt: (b, 0, 0)),
        scratch_shapes=[pltpu.VMEM((SUBLANE, W_BR), F32), pltpu.VMEM((1, LANE), F32)],
    )
    out = pl.pallas_call(
        kern,
        grid_spec=grid_spec,
        out_shape=jax.ShapeDtypeStruct((nb, 1, W_BR), BF16),
        compiler_params=_params(("arbitrary", "arbitrary")),
        name="sb_sample",
    )(page_table, q3, g3, bias_row, jnp.asarray(seg, BF16), jnp.asarray(expand, BF16),
      jnp.asarray(upper, BF16), *([cache_k] * PAGES_PER_STEP), *([cache_v] * PAGES_PER_STEP))
    return out.reshape(nb, W_BR)


def _head_ones(width):
    idx = np.arange(width) // N_D
    return jnp.asarray(idx[:, None] == idx[None, :], dtype=BF16)


def _rwkv_prep_kernel(r_ref, k_ref, v_ref, wa_ref, pr_ref, pk_ref, pv_ref, pwa_ref,
                      mur_ref, muk_ref, muv_ref, muwa_ref, w0_ref, w2_ref, a0_ref, a2_ref,
                      kk_ref_, ka_ref, rk_ref, eh_ref,
                      ro_ref, ko_ref, vo_ref, dec_ref, kko_ref, bo_ref, bon_ref, *, roll_rows, first_zero):
    def shifted(x_ref, p_ref, mu_ref):
        x = x_ref[...]
        if roll_rows:
            row = lax.broadcasted_iota(jnp.int32, x.shape, 0)
            last = p_ref[SUBLANE - 1:SUBLANE, :]
            if first_zero:
                last = jnp.where(pl.program_id(1) == 0, 0.0, last)
            prev = jnp.where(row == 0, last, pltpu.roll(x, 1, 0))
        else:
            prev = p_ref[...]
        return x + (prev - x) * mu_ref[...]

    r = shifted(r_ref, pr_ref, mur_ref)
    k = shifted(k_ref, pk_ref, muk_ref)
    v = shifted(v_ref, pv_ref, muv_ref)
    wa = shifted(wa_ref, pwa_ref, muwa_ref)
    eh = eh_ref[...]
    w = -_softplus(-(w0_ref[...] + _bdot(jnp.tanh(wa), w2_ref[...]))) - 0.5
    a = jax.nn.sigmoid(a0_ref[...] + _bdot(wa, a2_ref[...]))
    kk = k * kk_ref_[...]
    kk = kk * lax.rsqrt(_split_dot(kk * kk, eh, 3) + 1e-12)
    k2 = k * (1.0 + (a - 1.0) * ka_ref[...])
    ro_ref[...] = r
    ko_ref[...] = k2
    vo_ref[...] = v
    dec_ref[...] = jnp.exp(-jnp.exp(w))
    kko_ref[...] = kk
    bo_ref[...] = kk * a
    bon_ref[...] = _split_dot(r * k2 * rk_ref[...], eh, 3) * v


def _rwkv_prep(z, prev, nb, t_len, mu, w0, w2, a0, a2, k_k, k_a, r_k):
    n = z.shape[0]
    tm = min(512, t_len) if prev is None else n
    bps = max(t_len // tm, 1)
    wa_w = LR_W + LR_A
    row = lambda v: v.reshape(1, -1)
    col = lambda c, w: pl.BlockSpec((tm, w), lambda b, i: (b * bps + i, c // w))
    if prev is None:
        tsub = tm // SUBLANE
        pcol = lambda c, w: pl.BlockSpec(
            (SUBLANE, w), lambda b, i: (jnp.maximum((b * bps + i) * tsub - 1, 0), c // w))
        prev_specs = [pcol(COL_D_R, W_BR), pcol(COL_D_K, W_BR), pcol(COL_D_V, W_BR), pcol(COL_D_WA, wa_w)]
        prev_args = [z, z, z, z]
        grid = (nb, bps)
    else:
        pfull = lambda w: pl.BlockSpec((tm, w), lambda b, i: (0, 0))
        prev_specs = [pfull(W_BR), pfull(W_BR), pfull(W_BR), pfull(wa_w)]
        prev_args = [prev[:, 0:W_BR], prev[:, W_BR:2 * W_BR], prev[:, 2 * W_BR:3 * W_BR], prev[:, 3 * W_BR:]]
        grid = (1, 1)
    full = lambda shape: pl.BlockSpec(shape, lambda b, i: (0,) * len(shape))
    w2p = jnp.concatenate([w2, jnp.zeros((LR_A, W_BR), F32)], axis=0).astype(BF16)
    a2p = jnp.concatenate([jnp.zeros((LR_W, W_BR), F32), a2], axis=0).astype(BF16)
    kern = functools.partial(_rwkv_prep_kernel, roll_rows=prev is None, first_zero=prev is None)
    out = jax.ShapeDtypeStruct((n, W_BR), F32)
    ospec = pl.BlockSpec((tm, W_BR), lambda b, i: (b * bps + i, 0))
    return pl.pallas_call(
        kern,
        grid=grid,
        in_specs=[col(COL_D_R, W_BR), col(COL_D_K, W_BR), col(COL_D_V, W_BR), col(COL_D_WA, wa_w)]
                 + prev_specs
                 + [full((1, W_BR))] * 3 + [full((1, wa_w))]
                 + [full((1, W_BR)), full((wa_w, W_BR)), full((1, W_BR)), full((wa_w, W_BR))]
                 + [full((1, W_BR))] * 3 + [full((W_BR, W_BR))],
        out_specs=[ospec] * 7,
        out_shape=(out,) * 7,
        compiler_params=_params(("arbitrary", "arbitrary")),
        name="rwkv_prep",
    )(z, z, z, z, *prev_args,
      row(mu[0:W_BR]), row(mu[W_BR:2 * W_BR]), row(mu[2 * W_BR:3 * W_BR]), row(mu[3 * W_BR:]),
      row(w0), w2p, row(a0), a2p, row(k_k), row(k_a), row(r_k), _head_ones(W_BR))


def _rwkv_scan_kernel(r_ref, k_ref, v_ref, dec_ref, kk_ref, b_ref, bon_ref, g_ref, s0_ref,
                      lng_ref, lnb_ref, e2_ref, eh_ref, y_ref, sT_ref, s_scr, y_scr,
                      *, nb, lc, state_per_row):
    n_hp = H_D // 2
    c = pl.program_id(0)

    @pl.when(c == 0)
    def _():
        s_scr[...] = s0_ref[...]

    y_scr[...] = jnp.zeros_like(y_scr)
    e2 = e2_ref[...]
    sub = lax.broadcasted_iota(jnp.int32, (N_D, LANE), 0)
    lane = lax.broadcasted_iota(jnp.int32, (N_D, LANE), 1)
    diag = (lane % N_D) == sub

    sub8 = lax.broadcasted_iota(jnp.int32, (SUBLANE, LANE), 0)

    def one(b, t, hp):
        cols = slice(hp * LANE, (hp + 1) * LANE)
        t8 = pl.multiple_of((t // SUBLANE) * SUBLANE, SUBLANE)
        tr = t % SUBLANE
        rows8 = pl.ds(t8, SUBLANE)
        ld = lambda ref: pltpu.roll(ref[b, rows8, cols], (SUBLANE - tr) % SUBLANE, 0)[0:1, :]
        kk, dec, bb, k, r, v = ld(kk_ref), ld(dec_ref), ld(b_ref), ld(k_ref), ld(r_ref), ld(v_ref)
        idx = (t if state_per_row else b) * n_hp + hp
        s = s_scr[idx]
        sa = -_split_dot(s * kk, e2, 3)
        vb = _split_dot(jnp.where(diag, v, 0.0), e2, 3)
        s = s * dec + sa * bb + vb * k
        s_scr[idx] = s
        yrep = _split_dot(s * r, e2, 3)
        y_row = jnp.sum(jnp.where(diag, yrep, 0.0), axis=0, keepdims=True)
        y_scr[b, rows8, cols] = jnp.where(sub8 == tr, y_row, y_scr[b, rows8, cols])

    def step(t, carry):
        for b in range(nb):
            for hp in range(n_hp):
                one(b, t, hp)
        return carry

    lax.fori_loop(0, lc, step, 0)

    sT_ref[...] = s_scr[...]
    eh = eh_ref[...]
    for b in range(nb):
        y = y_scr[b]
        ym = _split_dot(y, eh, 3) * (1.0 / N_D)
        yc = y - ym
        yv = _split_dot(yc * yc, eh, 3) * (1.0 / N_D)
        out = yc * lax.rsqrt(yv + GN_EPS) * lng_ref[...] + lnb_ref[...] + bon_ref[b]
        y_ref[b] = (out * _silu(g_ref[b])).astype(y_ref.dtype)


def _rwkv_scan(prep, z3, s0, lnx_g, lnx_b, state_per_row):
    nb, t_len, _ = z3.shape
    lc = t_len if state_per_row else min(64, t_len)
    p3 = [p.reshape(nb, t_len, W_BR) for p in prep]
    blk = pl.BlockSpec((nb, lc, W_BR), lambda c: (0, c, 0))
    full = lambda shape: pl.BlockSpec(shape, lambda c: (0,) * len(shape))
    kern = functools.partial(_rwkv_scan_kernel, nb=nb, lc=lc, state_per_row=state_per_row)
    n_st = s0.shape[0]
    return pl.pallas_call(
        kern,
        grid=(t_len // lc,),
        in_specs=[blk] * 7
                 + [pl.BlockSpec((nb, lc, W_BR), lambda c: (0, c, COL_D_G // W_BR)),
                    full((n_st, N_D, LANE)), full((1, W_BR)), full((1, W_BR)),
                    full((LANE, LANE)), full((W_BR, W_BR))],
        out_specs=[blk, full((n_st, N_D, LANE))],
        out_shape=(jax.ShapeDtypeStruct((nb, t_len, W_BR), BF16),
                   jax.ShapeDtypeStruct((n_st, N_D, LANE), F32)),
        scratch_shapes=[pltpu.VMEM((n_st, N_D, LANE), F32), pltpu.VMEM((nb, lc, W_BR), F32)],
        compiler_params=_params(("arbitrary",)),
        name="rwkv_scan",
    )(p3[0], p3[1], p3[2], p3[3], p3[4], p3[5], p3[6], z3, s0,
      lnx_g.reshape(1, W_BR), lnx_b.reshape(1, W_BR), _head_ones(LANE), _head_ones(W_BR))


def _wkv_to_pairs(s):
    nb = s.shape[0]
    return s.reshape(nb, H_D // 2, 2, N_D, N_D).transpose(0, 1, 3, 2, 4).reshape(nb * H_D // 2, N_D, LANE)


def _wkv_from_pairs(s, nb):
    return s.reshape(nb, H_D // 2, N_D, 2, N_D).transpose(0, 1, 3, 2, 4).reshape(nb, H_D, N_D, N_D)


def _merge_kernel(x_ref, ya_ref, yb_ref, yc_ref, yd_ref, mga_ref, mgb_ref, mgc_ref, mgd_ref,
                  gt_ref, wb_ref, wo_ref, fg_ref, *out_refs, final):
    merged = None
    branches = ((ya_ref, mga_ref), (yb_ref, mgb_ref), (yc_ref, mgc_ref), (yd_ref, mgd_ref))
    for n, (y_ref, mg_ref) in enumerate(branches):
        proj = jnp.dot(y_ref[...].astype(BF16), wb_ref[n], preferred_element_type=F32)
        term = jax.nn.sigmoid(mg_ref[...]) * proj
        merged = term if merged is None else merged + term
    x_new = x_ref[...] + gt_ref[...] * _bdot(merged, wo_ref[...])
    out_refs[0][...] = x_new
    if final:
        ms = jnp.mean(x_new * x_new, axis=-1, keepdims=True)
        out_refs[1][...] = x_new * lax.rsqrt(ms + RMS_EPS) * fg_ref[...]


def _merge(x, ys, z, mod, w_branch, w_out, final_g, rows_per_group, final):
    n = x.shape[0]
    tm = min(256, n)
    r = mod.shape[1]
    bpg = rows_per_group // tm
    full = lambda shape: pl.BlockSpec(shape, lambda i: (0,) * len(shape))
    yspec = pl.BlockSpec((tm, W_BR), lambda i: (i, 0))
    xspec = pl.BlockSpec((tm, D_MODEL), lambda i: (i, 0))
    out_sds = jax.ShapeDtypeStruct((n, D_MODEL), F32)
    outs = pl.pallas_call(
        functools.partial(_merge_kernel, final=final),
        grid=(n // tm,),
        in_specs=[xspec, yspec, yspec, yspec, yspec]
                 + [pl.BlockSpec((tm, D_MODEL), lambda i, n=n: (i, COL_M_G // D_MODEL + n)) for n in range(N_BR)]
                 + [pl.BlockSpec((None, r, D_MODEL), lambda i: (i // bpg, 0, 2)),
                  full((N_BR, W_BR, D_MODEL)), full((D_MODEL, D_MODEL)), full((1, D_MODEL))],
        out_specs=[xspec, xspec] if final else [xspec],
        out_shape=(out_sds, out_sds) if final else (out_sds,),
        compiler_params=_params(("arbitrary",)),
        name="merge",
    )(x, *ys, z, z, z, z, mod, w_branch, w_out, final_g.reshape(1, D_MODEL))
    return outs


def _pad_w_in(w_in):
    depth = w_in.shape[0]
    pad = jnp.zeros((depth, D_MODEL, COL_D_G - IN_W_SRC_D_G), w_in.dtype)
    return jnp.concatenate([w_in[:, :, :IN_W_SRC_D_G], pad, w_in[:, :, IN_W_SRC_D_G:]], axis=2).astype(BF16)


def kernel(x_prompt, x_sample, cache_k, cache_v, state_ssm_re, state_ssm_im, state_wkv, state_shift, page_table, c_prompt, c_sample, norm_g, w_ada, b_ada, w_in, ssm_a_re, ssm_a_im, ssm_log_dt, ssm_b_re, ssm_b_im, ssm_c_re, ssm_c_im, ssm_d, ssm_w_glu, ssm_b_glu, sgu_ln_g, sgu_ln_b, sgu_w, sgu_b, sb_bias, rwkv_mu, rwkv_w0, rwkv_w2, rwkv_a0, rwkv_a2, rwkv_k_k, rwkv_k_a, rwkv_r_k, rwkv_lnx_g, rwkv_lnx_b, w_branch, w_out, final_norm_g):
    bp, t_len, _ = x_prompt.shape
    db = x_sample.shape[0]
    depth = w_in.shape[0]
    n_pool, page = cache_k.shape[1], cache_k.shape[2]
    np_rows = bp * t_len

    n_c = bp + db
    c_rows = -(-n_c // SUBLANE) * SUBLANE
    c_all = jnp.concatenate([c_prompt, c_sample, jnp.zeros((c_rows - n_c, D_MODEL), F32)], axis=0)
    mod = _ada_mod(c_all, w_ada, b_ada)
    w_in_p = _pad_w_in(w_in)
    w_branch_b = w_branch.astype(BF16)
    w_out_b = w_out.astype(BF16)
    w_glu_b = ssm_w_glu.astype(BF16)
    ck = cache_k.reshape(depth * n_pool, page, W_BR)
    cv = cache_v.reshape(depth * n_pool, page, W_BR)

    xp = x_prompt.reshape(np_rows, D_MODEL)
    xs = x_sample.reshape(db, D_MODEL)
    zero_wkv = jnp.zeros((bp * H_D // 2, N_D, LANE), F32)
    outs_p, outs_s = [], []
    yp = ys_out = None
    for l in range(depth):
        final = l == depth - 1
        mod_p = mod[l, :bp].reshape(bp, 1, 3 * D_MODEL)
        mod_s = mod[l, bp:bp + db].reshape(1, db, 3 * D_MODEL)
        abr, abi, bbr, bbi = _s5_discretise(ssm_a_re[l], ssm_a_im[l], ssm_log_dt[l], ssm_b_re[l], ssm_b_im[l])
        dense = _s5_dense_weights(abr, abi, bbr, bbi, ssm_c_re[l], ssm_c_im[l], bp)
        rw = (rwkv_mu[l], rwkv_w0[l], rwkv_w2[l], rwkv_a0[l], rwkv_a2[l], rwkv_k_k[l], rwkv_k_a[l],
              rwkv_r_k[l].reshape(W_BR))

        z = _in_proj(xp, mod_p, norm_g[l], w_in_p[l], t_len)
        z3 = z.reshape(bp, t_len, IN_W_PAD)
        ya, hT = _s5_prompt(z3, dense, ssm_d[l], w_glu_b[l], ssm_b_glu[l])
        yb = _sgu_prompt(z, sgu_ln_g[l], sgu_ln_b[l], sgu_w[l], sgu_b[l])
        yc = _sb_prompt(z, sb_bias[l], bp, t_len)
        prep = _rwkv_prep(z, None, bp, t_len, *rw)
        yd, sT = _rwkv_scan(prep, z3, zero_wkv, rwkv_lnx_g[l], rwkv_lnx_b[l], False)
        res = _merge(xp, (ya.reshape(np_rows, W_BR), yb, yc, yd.reshape(np_rows, W_BR)), z, mod_p,
                     w_branch_b[l], w_out_b[l], final_norm_g, t_len, final)
        xp = res[0]
        if final:
            yp = res[1]
        outs_p.append((
            z3[:, :, COL_C_K:COL_C_K + W_BR].reshape(bp, t_len, H_C, HD_C),
            z3[:, :, COL_C_V:COL_C_V + W_BR].reshape(bp, t_len, H_C, HD_C),
            hT[:bp].reshape(bp, G_A, P_A), hT[bp:].reshape(bp, G_A, P_A),
            _wkv_from_pairs(sT, bp),
            z3[:, -1, COL_D_R:COL_D_R + SHIFT_W],
        ))

        zs = _in_proj(xs, mod_s, norm_g[l], w_in_p[l], db)
        sa, hr, hi = _s5_sample(zs, state_ssm_re[l].reshape(db, S5_W), state_ssm_im[l].reshape(db, S5_W),
                                dense, abr, abi, ssm_d[l], w_glu_b[l], ssm_b_glu[l])
        sb, v_rows = _sgu_sample(zs, sgu_ln_g[l], sgu_ln_b[l], sgu_w[l], sgu_b[l])
        sc = _sb_sample(zs, ck, cv, page_table, sb_bias[l], l * n_pool)
        preps = _rwkv_prep(zs, state_shift[l], db, 1, *rw)
        sd, sTs = _rwkv_scan(preps, zs.reshape(1, db, IN_W_PAD), _wkv_to_pairs(state_wkv[l]),
                             rwkv_lnx_g[l], rwkv_lnx_b[l], True)
        res = _merge(xs, (sa, sb, sc, sd.reshape(db, W_BR)), zs, mod_s,
                     w_branch_b[l], w_out_b[l], final_norm_g, db, final)
        xs = res[0]
        if final:
            ys_out = res[1]
        outs_s.append((
            zs[:, COL_C_K:COL_C_K + W_BR].reshape(db, 1, H_C, HD_C),
            zs[:, COL_C_V:COL_C_V + W_BR].reshape(db, 1, H_C, HD_C),
            hr.reshape(db, G_A, P_A), hi.reshape(db, G_A, P_A),
            _wkv_from_pairs(sTs, db),
            zs[:, COL_D_R:COL_D_R + SHIFT_W],
            v_rows.reshape(db, 1, W_BR),
        ))

    stk = lambda outs, i: jnp.stack([o[i] for o in outs])
    return (yp.reshape(bp, t_len, D_MODEL), ys_out.reshape(db, 1, D_MODEL),
            stk(outs_p, 0), stk(outs_p, 1), stk(outs_s, 0), stk(outs_s, 1),
            stk(outs_p, 2), stk(outs_p, 3), stk(outs_s, 2), stk(outs_s, 3),
            stk(outs_p, 4), stk(outs_s, 4), stk(outs_p, 5), stk(outs_s, 5), stk(outs_s, 6))
```

```python
import functools
import math

import jax
import jax.numpy as jnp
import numpy as np
from jax import lax
from jax.experimental import pallas as pl
from jax.experimental.pallas import tpu as pltpu

F32 = jnp.float32
BF16 = jnp.bfloat16

D_MODEL = 1024
W_BR = D_MODEL // 2
N_BR = 4
CG_A = 16
G_A = W_BR // CG_A
P_A = 64
S5_W = G_A * P_A
S5_BLOCKS = 2
CHUNK = 128
HG_B = 4
H_C = 8
HD_C = W_BR // H_C
H_D = 8
N_D = W_BR // H_D
LR_W = 64
LR_A = 64
SHIFT_W = 3 * W_BR + LR_W + LR_A
GN_EPS = 64e-5
RMS_EPS = 1e-6
LN_EPS = 1e-5
LANE = 128
SUBLANE = 8

COL_A_U, COL_A_G = 0, 512
COL_B_U, COL_B_V, COL_B_G = 1024, 1536, 2048
COL_C_Q, COL_C_K, COL_C_V, COL_C_G = 2560, 3072, 3584, 4096
COL_D_R, COL_D_K, COL_D_V, COL_D_WA = 4608, 5120, 5632, 6144
COL_D_G = 6656
COL_M_G = 7168
IN_W_PAD = COL_M_G + N_BR * D_MODEL
IN_W_SRC_D_G = COL_D_WA + LR_W + LR_A

VMEM_LIMIT = 56 * 2**20


def _params(sem, vmem=VMEM_LIMIT):
    return pltpu.CompilerParams(dimension_semantics=sem, vmem_limit_bytes=vmem)


def _silu(x):
    return x * jax.nn.sigmoid(x)


def _gelu(x):
    c = math.sqrt(2.0 / math.pi)
    return 0.5 * x * (1.0 + jnp.tanh(c * (x + 0.044715 * (x * x * x))))


def _softplus(x):
    return jnp.maximum(x, 0.0) + jnp.log(1.0 + jnp.exp(-jnp.abs(x)))


def _bdot(a, b):
    return jnp.dot(a.astype(BF16), b.astype(BF16), preferred_element_type=F32)


def _split_dot(x, w, passes):
    pieces = []
    rem = x
    for p in range(passes):
        piece = rem.astype(BF16)
        pieces.append(piece)
        if p + 1 < passes:
            rem = rem - piece.astype(F32)
    if passes == 1:
        return jnp.dot(pieces[0], w, preferred_element_type=F32)
    return jnp.dot(jnp.concatenate(pieces, axis=1), jnp.concatenate([w] * passes, axis=0),
                   preferred_element_type=F32)


def _ada_kernel(c_ref, w_ref, b_ref, o_ref):
    o_ref[...] = _bdot(_silu(c_ref[...]), w_ref[...]) + b_ref[...]


def _ada_mod(c_all, w_ada, b_ada):
    depth = w_ada.shape[0]
    rows = c_all.shape[0]
    return pl.pallas_call(
        _ada_kernel,
        grid=(depth, 3),
        in_specs=[
            pl.BlockSpec((rows, D_MODEL), lambda l, k: (0, 0)),
            pl.BlockSpec((None, D_MODEL, D_MODEL), lambda l, k: (l, 0, k)),
            pl.BlockSpec((None, 1, D_MODEL), lambda l, k: (l, 0, k)),
        ],
        out_specs=pl.BlockSpec((None, rows, D_MODEL), lambda l, k: (l, 0, k)),
        out_shape=jax.ShapeDtypeStruct((depth, rows, 3 * D_MODEL), F32),
        compiler_params=_params(("arbitrary", "arbitrary")),
        name="ada_mod",
    )(c_all, w_ada, b_ada.reshape(depth, 1, 3 * D_MODEL))


def _inproj_kernel(x_ref, g_ref, sh_ref, sc_ref, w_ref, z_ref, h_scr):
    @pl.when(pl.program_id(1) == 0)
    def _():
        x = x_ref[...]
        ms = jnp.mean(x * x, axis=-1, keepdims=True)
        y = x * lax.rsqrt(ms + RMS_EPS) * g_ref[...]
        h_scr[...] = (y * (1.0 + sc_ref[...]) + sh_ref[...]).astype(BF16)

    z_ref[...] = jnp.dot(h_scr[...], w_ref[...], preferred_element_type=F32)


def _in_proj(x, mod, norm_g, w_in, rows_per_group):
    n = x.shape[0]
    tm = min(1024, rows_per_group)
    tn = 1024
    r = mod.shape[1]
    bpg = rows_per_group // tm
    mod_spec = lambda k: pl.BlockSpec((None, r, D_MODEL), lambda i, j: (i // bpg, 0, k))
    return pl.pallas_call(
        _inproj_kernel,
        grid=(n // tm, IN_W_PAD // tn),
        in_specs=[
            pl.BlockSpec((tm, D_MODEL), lambda i, j: (i, 0)),
            pl.BlockSpec((1, D_MODEL), lambda i, j: (0, 0)),
            mod_spec(0),
            mod_spec(1),
            pl.BlockSpec((D_MODEL, tn), lambda i, j: (0, j)),
        ],
        out_specs=pl.BlockSpec((tm, tn), lambda i, j: (i, j)),
        out_shape=jax.ShapeDtypeStruct((n, IN_W_PAD), F32),
        scratch_shapes=[pltpu.VMEM((tm, D_MODEL), BF16)],
        compiler_params=_params(("arbitrary", "arbitrary")),
        name="in_proj",
    )(x, norm_g.reshape(1, D_MODEL), mod, mod, w_in)


def _s5_disc_kernel(ar_ref, ai_ref, ldt_ref, br_ref, bi_ref,
                    abr_ref, abi_ref, bbr_ref, bbi_ref):
    ar, ai = ar_ref[...], ai_ref[...]
    dt = jnp.exp(ldt_ref[...])
    mag = jnp.exp(ar * dt)
    abar_re, abar_im = mag * jnp.cos(ai * dt), mag * jnp.sin(ai * dt)
    den = ar * ar + ai * ai
    xr, xi = abar_re - 1.0, abar_im
    coef_re = (xr * ar + xi * ai) / den
    coef_im = (xi * ar - xr * ai) / den
    br, bi = br_ref[...], bi_ref[...]
    abr_ref[...] = abar_re
    abi_ref[...] = abar_im
    bbr_ref[...] = coef_re * br - coef_im * bi
    bbi_ref[...] = coef_re * bi + coef_im * br


def _s5_discretise(a_re, a_im, log_dt, b_re, b_im):
    gp = jax.ShapeDtypeStruct((G_A, 1, P_A), F32)
    gcp = jax.ShapeDtypeStruct((G_A, CG_A, P_A), F32)
    return pl.pallas_call(
        _s5_disc_kernel,
        out_shape=(gp, gp, gcp, gcp),
        name="s5_discretise",
    )(a_re.reshape(G_A, 1, P_A), a_im.reshape(G_A, 1, P_A), log_dt.reshape(G_A, 1, 1),
      jnp.swapaxes(b_re, 1, 2), jnp.swapaxes(b_im, 1, 2))


def _s5_dense_weights(abar_re, abar_im, bbar_re, bbar_im, c_re, c_im, nb):
    eye = jnp.eye(G_A, dtype=F32)
    bd = lambda m: jnp.einsum('gcp,gh->gchp', m, eye).reshape(W_BR, S5_W).astype(BF16)
    cd = lambda m: jnp.einsum('gcp,gh->gphc', m, eye).reshape(S5_W, W_BR).astype(BF16)
    ar = abar_re.reshape(1, S5_W)
    ai = abar_im.reshape(1, S5_W)
    a1 = jnp.broadcast_to(ar, (2 * nb, S5_W))
    a2 = jnp.concatenate([jnp.broadcast_to(-ai, (nb, S5_W)), jnp.broadcast_to(ai, (nb, S5_W))], axis=0)
    cdr, cdi = cd(c_re), cd(c_im)
    cw, sw = W_BR // S5_BLOCKS, S5_W // S5_BLOCKS
    ccat = jnp.stack([jnp.concatenate([cdr[f * sw:(f + 1) * sw, f * cw:(f + 1) * cw],
                                       -cdi[f * sw:(f + 1) * sw, f * cw:(f + 1) * cw]], axis=0)
                      for f in range(S5_BLOCKS)])
    return bd(bbar_re), bd(bbar_im), cdr, cdi, a1, a2, ccat


def _s5_tail(y, u, g, d_ref, wglu_ref, bglu_ref):
    y = y + d_ref[...] * u
    y = y * jax.nn.sigmoid(_bdot(_gelu(y), wglu_ref[...]) + bglu_ref[...])
    return y * _silu(g)


def _s5_prompt_kernel(u_ref, g_ref, bdr_ref, bdi_ref, ccat_ref, a1_ref, a2_ref,
                      d_ref, wglu_ref, bglu_ref, y_ref, hT_ref, hs_scr, h_scr, *, nb, lc):
    c = pl.program_id(0)

    @pl.when(c == 0)
    def _():
        h_scr[...] = jnp.zeros_like(h_scr)

    rows = nb * lc
    n_cg = S5_W // LANE
    u = u_ref[...].reshape(rows, W_BR)
    ub = u.astype(BF16)
    cw, sw = W_BR // S5_BLOCKS, S5_W // S5_BLOCKS
    for half, bd_ref in enumerate((bdr_ref, bdi_ref)):
        for f in range(S5_BLOCKS):
            bu = jnp.dot(ub[:, f * cw:(f + 1) * cw], bd_ref[f * cw:(f + 1) * cw, f * sw:(f + 1) * sw],
                         preferred_element_type=F32)
            for c in range(sw // LANE):
                cg = f * (sw // LANE) + c
                hs_scr[cg, half * rows:(half + 1) * rows, :] = bu[:, c * LANE:(c + 1) * LANE]
    a1 = [a1_ref[cg] for cg in range(n_cg)]
    a2 = [a2_ref[cg] for cg in range(n_cg)]

    def step(t, hs):
        idx = pl.ds(t, 2 * nb, stride=lc)
        out = []
        for cg in range(n_cg):
            h = a1[cg] * hs[cg] + a2[cg] * pltpu.roll(hs[cg], nb, 0) + hs_scr[cg, idx, :]
            hs_scr[cg, idx, :] = h
            out.append(h)
        return tuple(out)

    hs = lax.fori_loop(0, lc, step, tuple(h_scr[cg] for cg in range(n_cg)))
    for cg in range(n_cg):
        h_scr[cg] = hs[cg]
        hT_ref[cg] = hs[cg]
    ys = []
    for f in range(S5_BLOCKS):
        cgs = range(f * (sw // LANE), (f + 1) * (sw // LANE))
        h_re = jnp.concatenate([hs_scr[cg, 0:rows, :] for cg in cgs], axis=1).astype(BF16)
        h_im = jnp.concatenate([hs_scr[cg, rows:2 * rows, :] for cg in cgs], axis=1).astype(BF16)
        ys.append(jnp.dot(jnp.concatenate([h_re, h_im], axis=1), ccat_ref[f], preferred_element_type=F32))
    y = jnp.concatenate(ys, axis=1)
    g = g_ref[...].reshape(rows, W_BR)
    out = _s5_tail(y, u, g, d_ref, wglu_ref, bglu_ref)
    y_ref[...] = out.reshape(nb, lc, W_BR).astype(y_ref.dtype)


def _s5_prompt(z3, dense, d, w_glu, b_glu):
    nb, t_len, _ = z3.shape
    assert 2 * nb == SUBLANE, "state rows [re; im] of all sequences must fill one sublane tile"
    lc = min(128, t_len)
    bdr, bdi, _, _, a1, a2, ccat = dense
    full = lambda shape: pl.BlockSpec(shape, lambda c: (0,) * len(shape))
    kern = functools.partial(_s5_prompt_kernel, nb=nb, lc=lc)
    n_cg = S5_W // LANE
    slabs = lambda a: a.reshape(2 * nb, n_cg, LANE).transpose(1, 0, 2)
    y, hT = pl.pallas_call(
        kern,
        grid=(t_len // lc,),
        in_specs=[
            pl.BlockSpec((nb, lc, W_BR), lambda c: (0, c, COL_A_U // W_BR)),
            pl.BlockSpec((nb, lc, W_BR), lambda c: (0, c, COL_A_G // W_BR)),
            full((W_BR, S5_W)), full((W_BR, S5_W)), full(ccat.shape),
            full((n_cg, 2 * nb, LANE)), full((n_cg, 2 * nb, LANE)),
            full((1, W_BR)), full((W_BR, W_BR)), full((1, W_BR)),
        ],
        out_specs=[
            pl.BlockSpec((nb, lc, W_BR), lambda c: (0, c, 0)),
            full((n_cg, 2 * nb, LANE)),
        ],
        out_shape=(jax.ShapeDtypeStruct((nb, t_len, W_BR), BF16),
                   jax.ShapeDtypeStruct((n_cg, 2 * nb, LANE), F32)),
        scratch_shapes=[pltpu.VMEM((n_cg, 2 * nb * lc, LANE), F32), pltpu.VMEM((n_cg, 2 * nb, LANE), F32)],
        compiler_params=_params(("arbitrary",)),
        name="s5_prompt",
    )(z3, z3, bdr, bdi, ccat, slabs(a1), slabs(a2), d.reshape(1, W_BR), w_glu, b_glu.reshape(1, W_BR))
    return y, hT.transpose(1, 0, 2).reshape(2 * nb, S5_W)


def _s5_sample_kernel(u_ref, g_ref, h0r_ref, h0i_ref, bdr_ref, bdi_ref, cdr_ref, cdi_ref,
                      ar_ref, ai_ref, d_ref, wglu_ref, bglu_ref, y_ref, hr_ref, hi_ref):
    u = u_ref[...]
    ub = u.astype(BF16)
    ar, ai = ar_ref[...], ai_ref[...]
    h0r, h0i = h0r_ref[...], h0i_ref[...]
    hr = jnp.dot(ub, bdr_ref[...], preferred_element_type=F32) + (ar * h0r - ai * h0i)
    hi = jnp.dot(ub, bdi_ref[...], preferred_element_type=F32) + (ar * h0i + ai * h0r)
    hr_ref[...] = hr
    hi_ref[...] = hi
    y = _bdot(hr, cdr_ref[...]) - _bdot(hi, cdi_ref[...])
    y_ref[...] = _s5_tail(y, u, g_ref[...], d_ref, wglu_ref, bglu_ref).astype(y_ref.dtype)


def _s5_sample(z, h0_re, h0_im, dense, abar_re, abar_im, d, w_glu, b_glu):
    nb = z.shape[0]
    bdr, bdi, cdr, cdi = dense[:4]
    full = lambda shape: pl.BlockSpec(shape, lambda i: (0,) * len(shape))
    st = jax.ShapeDtypeStruct((nb, S5_W), F32)
    return pl.pallas_call(
        _s5_sample_kernel,
        grid=(1,),
        in_specs=[
            pl.BlockSpec((nb, W_BR), lambda i: (0, COL_A_U // W_BR)),
            pl.BlockSpec((nb, W_BR), lambda i: (0, COL_A_G // W_BR)),
            full((nb, S5_W)), full((nb, S5_W)),
            full((W_BR, S5_W)), full((W_BR, S5_W)), full((S5_W, W_BR)), full((S5_W, W_BR)),
            full((1, S5_W)), full((1, S5_W)),
            full((1, W_BR)), full((W_BR, W_BR)), full((1, W_BR)),
        ],
        out_specs=[full((nb, W_BR)), full((nb, S5_W)), full((nb, S5_W))],
        out_shape=(jax.ShapeDtypeStruct((nb, W_BR), BF16), st, st),
        compiler_params=_params(("arbitrary",)),
        name="s5_sample",
    )(z, z, h0_re, h0_im, bdr, bdi, cdr, cdi, abar_re.reshape(1, S5_W), abar_im.reshape(1, S5_W),
      d.reshape(1, W_BR), w_glu, b_glu.reshape(1, W_BR))


def _layernorm(v, g_ref, b_ref):
    mu = jnp.mean(v, axis=-1, keepdims=True)
    vc = v - mu
    var = jnp.mean(vc * vc, axis=-1, keepdims=True)
    return vc * lax.rsqrt(var + LN_EPS) * g_ref[...] + b_ref[...]


def _sgu_prompt_kernel(u_ref, v_ref, g_ref, lng_ref, lnb_ref, ws_ref, bs_ref, y_ref, *, n_chunks):
    cw = W_BR // HG_B
    vn = _layernorm(_gelu(v_ref[...]), lng_ref, lnb_ref)
    row = lax.broadcasted_iota(jnp.int32, (CHUNK, CHUNK), 0)
    col = lax.broadcasted_iota(jnp.int32, (CHUNK, CHUNK), 1)
    ws = [jnp.where(col <= row, ws_ref[h], 0.0).astype(BF16) for h in range(HG_B)]
    bs = bs_ref[...]
    for c in range(n_chunks):
        rows = slice(c * CHUNK, (c + 1) * CHUNK)
        vb = vn[rows, :].astype(BF16)
        mix = jnp.concatenate(
            [jnp.dot(ws[h], vb[:, h * cw:(h + 1) * cw], preferred_element_type=F32) for h in range(HG_B)],
            axis=1) + bs
        out = _gelu(u_ref[rows, :]) * mix * _silu(g_ref[rows, :])
        y_ref[rows, :] = out.astype(y_ref.dtype)


def _sgu_prompt(z, ln_g, ln_b, w_s, b_s):
    n = z.shape[0]
    tm = min(512, n)
    full = lambda shape: pl.BlockSpec(shape, lambda i: (0,) * len(shape))
    bs_full = jnp.repeat(b_s.T, W_BR // HG_B, axis=1)
    kern = functools.partial(_sgu_prompt_kernel, n_chunks=tm // CHUNK)
    col = lambda c: pl.BlockSpec((tm, W_BR), lambda i: (i, c // W_BR))
    return pl.pallas_call(
        kern,
        grid=(n // tm,),
        in_specs=[col(COL_B_U), col(COL_B_V), col(COL_B_G),
                  full((1, W_BR)), full((1, W_BR)), full((HG_B, CHUNK, CHUNK)), full((CHUNK, W_BR))],
        out_specs=pl.BlockSpec((tm, W_BR), lambda i: (i, 0)),
        out_shape=jax.ShapeDtypeStruct((n, W_BR), BF16),
        compiler_params=_params(("arbitrary",)),
        name="sgu_prompt",
    )(z, z, z, ln_g.reshape(1, W_BR), ln_b.reshape(1, W_BR), w_s, bs_full)


def _sgu_sample_kernel(u_ref, v_ref, g_ref, lng_ref, lnb_ref, w00_ref, b0_ref, y_ref, vn_ref):
    vn = _layernorm(_gelu(v_ref[...]), lng_ref, lnb_ref)
    vn_ref[...] = vn
    mix = vn * w00_ref[...] + b0_ref[...]
    y_ref[...] = (_gelu(u_ref[...]) * mix * _silu(g_ref[...])).astype(y_ref.dtype)


def _sgu_sample(z, ln_g, ln_b, w_s, b_s):
    nb = z.shape[0]
    cw = W_BR // HG_B
    full = lambda shape: pl.BlockSpec(shape, lambda i: (0,) * len(shape))
    col = lambda c: pl.BlockSpec((nb, W_BR), lambda i: (0, c // W_BR))
    w00 = jnp.repeat(w_s[:, 0, 0], cw).reshape(1, W_BR)
    b0 = jnp.repeat(b_s[:, 0], cw).reshape(1, W_BR)
    return pl.pallas_call(
        _sgu_sample_kernel,
        grid=(1,),
        in_specs=[col(COL_B_U), col(COL_B_V), col(COL_B_G),
                  full((1, W_BR)), full((1, W_BR)), full((1, W_BR)), full((1, W_BR))],
        out_specs=[full((nb, W_BR)), full((nb, W_BR))],
        out_shape=(jax.ShapeDtypeStruct((nb, W_BR), BF16), jax.ShapeDtypeStruct((nb, W_BR), F32)),
        compiler_params=_params(("arbitrary",)),
        name="sgu_sample",
    )(z, z, z, ln_g.reshape(1, W_BR), ln_b.reshape(1, W_BR), w00, b0)


def _suffix_matrix():
    j = np.arange(LANE)[:, None]
    s = np.arange(LANE)[None, :]
    return jnp.asarray(np.concatenate([(j >= s), np.ones((LANE, LANE), bool)], axis=1), dtype=BF16)


LOG2E = 1.0 / math.log(2.0)


def _softplus2(z):
    return jnp.maximum(z, 0.0) + jnp.log2(1.0 + jnp.exp2(-jnp.abs(z)))


def _sb_prompt_kernel(it_ref, jt_ref, bias_ref, q_ref, k_ref, v_ref, g_ref, m_ref, o_ref,
                      acc0, acc1, car0, car1, *, tq, tk):
    hp = pl.program_id(1)
    p = pl.program_id(2)
    i = it_ref[p]
    j = jt_ref[p]
    nsub = tk // LANE
    accs, cars = (acc0, acc1), (car0, car1)

    @pl.when(j == 0)
    def _():
        for r in (acc0, acc1, car0, car1):
            r[...] = jnp.zeros_like(r)

    def process(masked):
        lane = lax.broadcasted_iota(jnp.int32, (tq, LANE), 1)
        q = q_ref[...] * (HD_C ** -0.5 * LOG2E)
        qh = (jnp.where(lane < HD_C, q, 0.0).astype(BF16), jnp.where(lane >= HD_C, q, 0.0).astype(BF16))
        m = m_ref[...]
        kb = k_ref[...].astype(BF16)
        vb = v_ref[...].astype(BF16)
        if masked:
            valid = (lax.broadcasted_iota(jnp.int32, (tq, tk), 1)
                     < lax.broadcasted_iota(jnp.int32, (tq, tk), 0))
        for h in range(2):
            s = lax.dot_general(qh[h], kb, (((1,), (1,)), ((), ())), preferred_element_type=F32)
            z = s + bias_ref[2 * hp + h] * LOG2E
            sp = _softplus2(z)
            if masked:
                sp = jnp.where(valid, sp, 0.0)
            spb = sp.astype(BF16)
            rs = [jnp.dot(spb[:, sub * LANE:(sub + 1) * LANE], m, preferred_element_type=F32)
                  for sub in range(nsub)]
            c = cars[h][...]
            es = [None] * nsub
            for sub in reversed(range(nsub)):
                es[sub] = jnp.exp2(z[:, sub * LANE:(sub + 1) * LANE] - rs[sub][:, :LANE] - c)
                c = c + rs[sub][:, LANE:]
            cars[h][...] = c
            e = jnp.concatenate(es, axis=1)
            if masked:
                e = jnp.where(valid, e, 0.0)
            accs[h][...] += jnp.dot(e.astype(BF16), vb, preferred_element_type=F32)

    @pl.when(j == 0)
    def _():
        process(True)

    @pl.when(j > 0)
    def _():
        process(False)

    @pl.when(j == i)
    def _():
        lane = lax.broadcasted_iota(jnp.int32, (tq, LANE), 1)
        y = jnp.where(lane < HD_C, acc0[...], acc1[...])
        o_ref[...] = (y * _silu(g_ref[...])).astype(o_ref.dtype)


def _sb_prompt(z, sb_bias, nb, t_len):
    tq = tk = min(512, t_len)
    nq = t_len // tq
    pairs = [(i, j) for i in range(nq) for j in range(i + 1)]
    it = jnp.asarray([p[0] for p in pairs], jnp.int32)
    jt = jnp.asarray([p[1] for p in pairs], jnp.int32)
    kern = functools.partial(_sb_prompt_kernel, tq=tq, tk=tk)
    qspec = lambda c: pl.BlockSpec((tq, LANE), lambda b, h, p, it, jt: (b * nq + it[p], c // LANE + h))
    kspec = lambda c: pl.BlockSpec((tk, LANE), lambda b, h, p, it, jt: (b * nq + it[p] - jt[p], c // LANE + h))
    grid_spec = pltpu.PrefetchScalarGridSpec(
        num_scalar_prefetch=2,
        grid=(nb, H_C // 2, len(pairs)),
        in_specs=[
            pl.BlockSpec(memory_space=pltpu.SMEM),
            qspec(COL_C_Q), kspec(COL_C_K), kspec(COL_C_V), qspec(COL_C_G),
            pl.BlockSpec((LANE, 2 * LANE), lambda b, h, p, it, jt: (0, 0)),
        ],
        out_specs=pl.BlockSpec((tq, LANE), lambda b, h, p, it, jt: (b * nq + it[p], h)),
        scratch_shapes=[pltpu.VMEM((tq, LANE), F32)] * 4,
    )
    return pl.pallas_call(
        kern,
        grid_spec=grid_spec,
        out_shape=jax.ShapeDtypeStruct((nb * t_len, W_BR), BF16),
        compiler_params=_params(("arbitrary",) * 3),
        name="sb_prompt",
    )(it, jt, sb_bias, z, z, z, z, _suffix_matrix())


PAGES_PER_STEP = 8


def _sb_sample_kernel(pt_ref, qb_ref, g_ref, bias_ref, m_ref, *rest):
    k_refs = rest[:PAGES_PER_STEP]
    v_refs = rest[PAGES_PER_STEP:2 * PAGES_PER_STEP]
    o_ref, acc, car = rest[2 * PAGES_PER_STEP:]
    jj = pl.program_id(1)

    @pl.when(jj == 0)
    def _():
        acc[...] = jnp.zeros_like(acc)
        car[...] = jnp.zeros_like(car)

    qb = qb_ref[...] * (HD_C ** -0.5 * LOG2E)
    bias = bias_ref[...] * LOG2E
    m = m_ref[...]
    for r in reversed(range(PAGES_PER_STEP)):
        prod = k_refs[r][...] * qb
        z = jnp.concatenate([jnp.sum(prod[h * HD_C:(h + 1) * HD_C, :], axis=0, keepdims=True)
                             for h in range(H_C)], axis=0) + bias
        rr = _split_dot(_softplus2(z), m, 2)
        e = jnp.exp2(z - rr[:, :LANE] - car[...])
        car[...] += rr[:, LANE:]
        eb = jnp.concatenate([jnp.broadcast_to(e[h:h + 1, :], (HD_C, LANE)) for h in range(H_C)], axis=0)
        acc[...] += v_refs[r][...] * eb

    @pl.when(jj == pl.num_programs(1) - 1)
    def _():
        o_ref[...] = jnp.sum(acc[...], axis=1, keepdims=True) * _silu(g_ref[...])


def _sb_sample(z, cache_k, cache_v, page_table, sb_bias, base):
    nb = z.shape[0]
    n_pages = page_table.shape[1]
    page = cache_k.shape[2]
    assert page == LANE and n_pages % PAGES_PER_STEP == 0
    n_steps = n_pages // PAGES_PER_STEP
    bias = jnp.broadcast_to(sb_bias[:, None], (H_C, page))
    qb = jnp.broadcast_to(z[:, COL_C_Q:COL_C_Q + W_BR, None], (nb, W_BR, page))
    g3 = z[:, COL_C_G:COL_C_G + W_BR].reshape(nb, W_BR, 1)

    def page_spec(r):
        def imap(b, jj, pt):
            return (base + pt[b, (n_steps - 1 - jj) * PAGES_PER_STEP + r], 0, 0)
        return pl.BlockSpec((None, W_BR, page), imap)

    const = lambda shape: pl.BlockSpec(shape, lambda b, jj, pt: (0,) * len(shape))
    col_spec = pl.BlockSpec((None, W_BR, 1), lambda b, jj, pt: (b, 0, 0))
    grid_spec = pltpu.PrefetchScalarGridSpec(
        num_scalar_prefetch=1,
        grid=(nb, n_steps),
        in_specs=[pl.BlockSpec((None, W_BR, page), lambda b, jj, pt: (b, 0, 0)), col_spec,
                  const((H_C, page)), const((LANE, 2 * LANE))]
                 + [page_spec(r) for r in range(PAGES_PER_STEP)]
                 + [page_spec(r) for r in range(PAGES_PER_STEP)],
        out_specs=col_spec,
        scratch_shapes=[pltpu.VMEM((W_BR, page), F32), pltpu.VMEM((H_C, LANE), F32)],
    )
    out = pl.pallas_call(
        _sb_sample_kernel,
        grid_spec=grid_spec,
        out_shape=jax.ShapeDtypeStruct((nb, W_BR, 1), F32),
        compiler_params=_params(("arbitrary", "arbitrary")),
        name="sb_sample",
    )(page_table, qb, g3, bias, _suffix_matrix(),
      *([cache_k] * PAGES_PER_STEP), *([cache_v] * PAGES_PER_STEP))
    return out.reshape(nb, W_BR)


def _head_ones(width):
    idx = np.arange(width) // N_D
    return jnp.asarray(idx[:, None] == idx[None, :], dtype=BF16)


def _rwkv_prep_kernel(r_ref, k_ref, v_ref, wa_ref, pr_ref, pk_ref, pv_ref, pwa_ref,
                      mur_ref, muk_ref, muv_ref, muwa_ref, w0_ref, w2_ref, a0_ref, a2_ref,
                      kk_ref_, ka_ref, rk_ref, eh_ref,
                      ro_ref, ko_ref, vo_ref, dec_ref, kko_ref, bo_ref, bon_ref, *, roll_rows, first_zero):
    def shifted(x_ref, p_ref, mu_ref):
        x = x_ref[...]
        if roll_rows:
            row = lax.broadcasted_iota(jnp.int32, x.shape, 0)
            last = p_ref[SUBLANE - 1:SUBLANE, :]
            if first_zero:
                last = jnp.where(pl.program_id(1) == 0, 0.0, last)
            prev = jnp.where(row == 0, last, pltpu.roll(x, 1, 0))
        else:
            prev = p_ref[...]
        return x + (prev - x) * mu_ref[...]

    r = shifted(r_ref, pr_ref, mur_ref)
    k = shifted(k_ref, pk_ref, muk_ref)
    v = shifted(v_ref, pv_ref, muv_ref)
    wa = shifted(wa_ref, pwa_ref, muwa_ref)
    eh = eh_ref[...]
    w = -_softplus(-(w0_ref[...] + _bdot(jnp.tanh(wa), w2_ref[...]))) - 0.5
    a = jax.nn.sigmoid(a0_ref[...] + _bdot(wa, a2_ref[...]))
    kk = k * kk_ref_[...]
    kk = kk * lax.rsqrt(_split_dot(kk * kk, eh, 3) + 1e-12)
    k2 = k * (1.0 + (a - 1.0) * ka_ref[...])
    ro_ref[...] = r
    ko_ref[...] = k2
    vo_ref[...] = v
    dec_ref[...] = jnp.exp(-jnp.exp(w))
    kko_ref[...] = kk
    bo_ref[...] = kk * a
    bon_ref[...] = _split_dot(r * k2 * rk_ref[...], eh, 3) * v


def _rwkv_prep(z, prev, nb, t_len, mu, w0, w2, a0, a2, k_k, k_a, r_k):
    n = z.shape[0]
    tm = min(512, t_len) if prev is None else n
    bps = max(t_len // tm, 1)
    wa_w = LR_W + LR_A
    row = lambda v: v.reshape(1, -1)
    col = lambda c, w: pl.BlockSpec((tm, w), lambda b, i: (b * bps + i, c // w))
    if prev is None:
        tsub = tm // SUBLANE
        pcol = lambda c, w: pl.BlockSpec(
            (SUBLANE, w), lambda b, i: (jnp.maximum((b * bps + i) * tsub - 1, 0), c // w))
        prev_specs = [pcol(COL_D_R, W_BR), pcol(COL_D_K, W_BR), pcol(COL_D_V, W_BR), pcol(COL_D_WA, wa_w)]
        prev_args = [z, z, z, z]
        grid = (nb, bps)
    else:
        pfull = lambda w: pl.BlockSpec((tm, w), lambda b, i: (0, 0))
        prev_specs = [pfull(W_BR), pfull(W_BR), pfull(W_BR), pfull(wa_w)]
        prev_args = [prev[:, 0:W_BR], prev[:, W_BR:2 * W_BR], prev[:, 2 * W_BR:3 * W_BR], prev[:, 3 * W_BR:]]
        grid = (1, 1)
    full = lambda shape: pl.BlockSpec(shape, lambda b, i: (0,) * len(shape))
    w2p = jnp.concatenate([w2, jnp.zeros((LR_A, W_BR), F32)], axis=0).astype(BF16)
    a2p = jnp.concatenate([jnp.zeros((LR_W, W_BR), F32), a2], axis=0).astype(BF16)
    kern = functools.partial(_rwkv_prep_kernel, roll_rows=prev is None, first_zero=prev is None)
    out = jax.ShapeDtypeStruct((n, W_BR), F32)
    ospec = pl.BlockSpec((tm, W_BR), lambda b, i: (b * bps + i, 0))
    return pl.pallas_call(
        kern,
        grid=grid,
        in_specs=[col(COL_D_R, W_BR), col(COL_D_K, W_BR), col(COL_D_V, W_BR), col(COL_D_WA, wa_w)]
                 + prev_specs
                 + [full((1, W_BR))] * 3 + [full((1, wa_w))]
                 + [full((1, W_BR)), full((wa_w, W_BR)), full((1, W_BR)), full((wa_w, W_BR))]
                 + [full((1, W_BR))] * 3 + [full((W_BR, W_BR))],
        out_specs=[ospec] * 7,
        out_shape=(out,) * 7,
        compiler_params=_params(("arbitrary", "arbitrary")),
        name="rwkv_prep",
    )(z, z, z, z, *prev_args,
      row(mu[0:W_BR]), row(mu[W_BR:2 * W_BR]), row(mu[2 * W_BR:3 * W_BR]), row(mu[3 * W_BR:]),
      row(w0), w2p, row(a0), a2p, row(k_k), row(k_a), row(r_k), _head_ones(W_BR))


def _rwkv_scan_kernel(r_ref, k_ref, v_ref, dec_ref, kk_ref, b_ref, bon_ref, g_ref, s0_ref,
                      lng_ref, lnb_ref, e2_ref, eh_ref, y_ref, sT_ref, s_scr, y_scr,
                      *, nb, lc, state_per_row):
    n_hp = H_D // 2
    c = pl.program_id(0)

    @pl.when(c == 0)
    def _():
        s_scr[...] = s0_ref[...]

    e2 = e2_ref[...]
    sub = lax.broadcasted_iota(jnp.int32, (N_D, LANE), 0)
    lane = lax.broadcasted_iota(jnp.int32, (N_D, LANE), 1)
    diag = (lane % N_D) == sub
    pack = 2 * SUBLANE
    diag_b = diag.astype(F32).astype(BF16).reshape(N_D // pack, pack, LANE)
    combos = [(b, hp) for b in range(nb) for hp in range(n_hp)]
    nc = len(combos)

    def group(gi, carry):
        r0 = pl.multiple_of(gi * SUBLANE, SUBLANE)
        rows8 = pl.ds(r0, SUBLANE)
        load = lambda ref: [ref[b, rows8, :] for b in range(nb)]
        rr, kx, vx, dx, kkx, bx = (load(x) for x in (r_ref, k_ref, v_ref, dec_ref, kk_ref, b_ref))
        row = lambda blocks, b, hp, tt: blocks[b][tt:tt + 1, hp * LANE:(hp + 1) * LANE]

        def vdiag(tt, b, hp, lo):
            v = row(vx, b, hp, tt)
            vh = v.astype(BF16).astype(F32)
            piece = jnp.broadcast_to(v - vh if lo else vh, (pack, LANE)).astype(BF16)
            return (diag_b * piece[None]).reshape(N_D, LANE)

        vlhs = [jnp.concatenate([vdiag(tt, b, hp, lo) for tt in range(SUBLANE) for (b, hp) in combos], axis=0)
                for lo in (False, True)]
        vb_all = jnp.dot(jnp.concatenate(vlhs, axis=1), jnp.concatenate([e2, e2], axis=0),
                         preferred_element_type=F32)

        y_rows = [[[None] * n_hp for _ in range(SUBLANE)] for _ in range(nb)]
        for tt in range(SUBLANE):
            idx = [((r0 + tt) if state_per_row else b) * n_hp + hp for (b, hp) in combos]
            s = [s_scr[i] for i in idx]
            p1 = jnp.concatenate([s[c] * (-row(kkx, b, hp, tt)) for c, (b, hp) in enumerate(combos)], axis=0)
            sa = _split_dot(p1, e2, 2)
            new = []
            for c, (b, hp) in enumerate(combos):
                vb = vb_all[(tt * nc + c) * N_D:(tt * nc + c + 1) * N_D, :]
                sn = (s[c] * row(dx, b, hp, tt) + sa[c * N_D:(c + 1) * N_D, :] * row(bx, b, hp, tt)
                      + vb * row(kx, b, hp, tt))
                s_scr[idx[c]] = sn
                new.append(sn)
            p2 = jnp.concatenate([new[c] * row(rr, b, hp, tt) for c, (b, hp) in enumerate(combos)], axis=0)
            yrep = _split_dot(p2, e2, 1)
            for c, (b, hp) in enumerate(combos):
                y_rows[b][tt][hp] = jnp.sum(jnp.where(diag, yrep[c * N_D:(c + 1) * N_D, :], 0.0),
                                            axis=0, keepdims=True)
        for b in range(nb):
            y_scr[b, rows8, :] = jnp.concatenate(
                [jnp.concatenate(y_rows[b][tt], axis=1) for tt in range(SUBLANE)], axis=0)
        return carry

    lax.fori_loop(0, lc // SUBLANE, group, 0)

    sT_ref[...] = s_scr[...]
    eh = eh_ref[...]
    for b in range(nb):
        y = y_scr[b]
        ym = _split_dot(y, eh, 3) * (1.0 / N_D)
        yc = y - ym
        yv = _split_dot(yc * yc, eh, 3) * (1.0 / N_D)
        out = yc * lax.rsqrt(yv + GN_EPS) * lng_ref[...] + lnb_ref[...] + bon_ref[b]
        y_ref[b] = (out * _silu(g_ref[b])).astype(y_ref.dtype)


def _rwkv_scan(prep, z3, s0, lnx_g, lnx_b, state_per_row):
    nb, t_len, _ = z3.shape
    lc = t_len if state_per_row else min(128, t_len)
    p3 = [p.reshape(nb, t_len, W_BR) for p in prep]
    blk = pl.BlockSpec((nb, lc, W_BR), lambda c: (0, c, 0))
    full = lambda shape: pl.BlockSpec(shape, lambda c: (0,) * len(shape))
    kern = functools.partial(_rwkv_scan_kernel, nb=nb, lc=lc, state_per_row=state_per_row)
    n_st = s0.shape[0]
    return pl.pallas_call(
        kern,
        grid=(t_len // lc,),
        in_specs=[blk] * 7
                 + [pl.BlockSpec((nb, lc, W_BR), lambda c: (0, c, COL_D_G // W_BR)),
                    full((n_st, N_D, LANE)), full((1, W_BR)), full((1, W_BR)),
                    full((LANE, LANE)), full((W_BR, W_BR))],
        out_specs=[blk, full((n_st, N_D, LANE))],
        out_shape=(jax.ShapeDtypeStruct((nb, t_len, W_BR), BF16),
                   jax.ShapeDtypeStruct((n_st, N_D, LANE), F32)),
        scratch_shapes=[pltpu.VMEM((n_st, N_D, LANE), F32), pltpu.VMEM((nb, lc, W_BR), F32)],
        compiler_params=_params(("arbitrary",)),
        name="rwkv_scan",
    )(p3[0], p3[1], p3[2], p3[3], p3[4], p3[5], p3[6], z3, s0,
      lnx_g.reshape(1, W_BR), lnx_b.reshape(1, W_BR), _head_ones(LANE), _head_ones(W_BR))


def _wkv_to_pairs(s):
    nb = s.shape[0]
    return s.reshape(nb, H_D // 2, 2, N_D, N_D).transpose(0, 1, 3, 2, 4).reshape(nb * H_D // 2, N_D, LANE)


def _wkv_from_pairs(s, nb):
    return s.reshape(nb, H_D // 2, N_D, 2, N_D).transpose(0, 1, 3, 2, 4).reshape(nb, H_D, N_D, N_D)


def _merge_kernel(x_ref, ya_ref, yb_ref, yc_ref, yd_ref, mga_ref, mgb_ref, mgc_ref, mgd_ref,
                  gt_ref, wb_ref, wo_ref, fg_ref, *out_refs, final):
    merged = None
    branches = ((ya_ref, mga_ref), (yb_ref, mgb_ref), (yc_ref, mgc_ref), (yd_ref, mgd_ref))
    for n, (y_ref, mg_ref) in enumerate(branches):
        proj = jnp.dot(y_ref[...].astype(BF16), wb_ref[n], preferred_element_type=F32)
        term = jax.nn.sigmoid(mg_ref[...]) * proj
        merged = term if merged is None else merged + term
    x_new = x_ref[...] + gt_ref[...] * _bdot(merged, wo_ref[...])
    out_refs[0][...] = x_new
    if final:
        ms = jnp.mean(x_new * x_new, axis=-1, keepdims=True)
        out_refs[1][...] = x_new * lax.rsqrt(ms + RMS_EPS) * fg_ref[...]


def _merge(x, ys, z, mod, w_branch, w_out, final_g, rows_per_group, final):
    n = x.shape[0]
    tm = min(256, n)
    r = mod.shape[1]
    bpg = rows_per_group // tm
    full = lambda shape: pl.BlockSpec(shape, lambda i: (0,) * len(shape))
    yspec = pl.BlockSpec((tm, W_BR), lambda i: (i, 0))
    xspec = pl.BlockSpec((tm, D_MODEL), lambda i: (i, 0))
    out_sds = jax.ShapeDtypeStruct((n, D_MODEL), F32)
    outs = pl.pallas_call(
        functools.partial(_merge_kernel, final=final),
        grid=(n // tm,),
        in_specs=[xspec, yspec, yspec, yspec, yspec]
                 + [pl.BlockSpec((tm, D_MODEL), lambda i, n=n: (i, COL_M_G // D_MODEL + n)) for n in range(N_BR)]
                 + [pl.BlockSpec((None, r, D_MODEL), lambda i: (i // bpg, 0, 2)),
                  full((N_BR, W_BR, D_MODEL)), full((D_MODEL, D_MODEL)), full((1, D_MODEL))],
        out_specs=[xspec, xspec] if final else [xspec],
        out_shape=(out_sds, out_sds) if final else (out_sds,),
        compiler_params=_params(("arbitrary",)),
        name="merge",
    )(x, *ys, z, z, z, z, mod, w_branch, w_out, final_g.reshape(1, D_MODEL))
    return outs


def _pad_w_in(w_in):
    depth = w_in.shape[0]
    pad = jnp.zeros((depth, D_MODEL, COL_D_G - IN_W_SRC_D_G), w_in.dtype)
    return jnp.concatenate([w_in[:, :, :IN_W_SRC_D_G], pad, w_in[:, :, IN_W_SRC_D_G:]], axis=2).astype(BF16)


def kernel(x_prompt, x_sample, cache_k, cache_v, state_ssm_re, state_ssm_im, state_wkv, state_shift, page_table, c_prompt, c_sample, norm_g, w_ada, b_ada, w_in, ssm_a_re, ssm_a_im, ssm_log_dt, ssm_b_re, ssm_b_im, ssm_c_re, ssm_c_im, ssm_d, ssm_w_glu, ssm_b_glu, sgu_ln_g, sgu_ln_b, sgu_w, sgu_b, sb_bias, rwkv_mu, rwkv_w0, rwkv_w2, rwkv_a0, rwkv_a2, rwkv_k_k, rwkv_k_a, rwkv_r_k, rwkv_lnx_g, rwkv_lnx_b, w_branch, w_out, final_norm_g):
    bp, t_len, _ = x_prompt.shape
    db = x_sample.shape[0]
    depth = w_in.shape[0]
    n_pool, page = cache_k.shape[1], cache_k.shape[2]
    np_rows = bp * t_len

    n_c = bp + db
    c_rows = -(-n_c // SUBLANE) * SUBLANE
    c_all = jnp.concatenate([c_prompt, c_sample, jnp.zeros((c_rows - n_c, D_MODEL), F32)], axis=0)
    mod = _ada_mod(c_all, w_ada, b_ada)
    w_in_p = _pad_w_in(w_in)
    w_branch_b = w_branch.astype(BF16)
    w_out_b = w_out.astype(BF16)
    w_glu_b = ssm_w_glu.astype(BF16)
    ck = cache_k.transpose(0, 1, 3, 4, 2).reshape(depth * n_pool, W_BR, page)
    cv = cache_v.transpose(0, 1, 3, 4, 2).reshape(depth * n_pool, W_BR, page)

    xp = x_prompt.reshape(np_rows, D_MODEL)
    xs = x_sample.reshape(db, D_MODEL)
    zero_wkv = jnp.zeros((bp * H_D // 2, N_D, LANE), F32)
    outs_p, outs_s = [], []
    yp = ys_out = None
    for l in range(depth):
        final = l == depth - 1
        mod_p = mod[l, :bp].reshape(bp, 1, 3 * D_MODEL)
        mod_s = mod[l, bp:bp + db].reshape(1, db, 3 * D_MODEL)
        abr, abi, bbr, bbi = _s5_discretise(ssm_a_re[l], ssm_a_im[l], ssm_log_dt[l], ssm_b_re[l], ssm_b_im[l])
        dense = _s5_dense_weights(abr, abi, bbr, bbi, ssm_c_re[l], ssm_c_im[l], bp)
        rw = (rwkv_mu[l], rwkv_w0[l], rwkv_w2[l], rwkv_a0[l], rwkv_a2[l], rwkv_k_k[l], rwkv_k_a[l],
              rwkv_r_k[l].reshape(W_BR))

        z = _in_proj(xp, mod_p, norm_g[l], w_in_p[l], t_len)
        z3 = z.reshape(bp, t_len, IN_W_PAD)
        ya, hT = _s5_prompt(z3, dense, ssm_d[l], w_glu_b[l], ssm_b_glu[l])
        yb = _sgu_prompt(z, sgu_ln_g[l], sgu_ln_b[l], sgu_w[l], sgu_b[l])
        yc = _sb_prompt(z, sb_bias[l], bp, t_len)
        prep = _rwkv_prep(z, None, bp, t_len, *rw)
        yd, sT = _rwkv_scan(prep, z3, zero_wkv, rwkv_lnx_g[l], rwkv_lnx_b[l], False)
        res = _merge(xp, (ya.reshape(np_rows, W_BR), yb, yc, yd.reshape(np_rows, W_BR)), z, mod_p,
                     w_branch_b[l], w_out_b[l], final_norm_g, t_len, final)
        xp = res[0]
        if final:
            yp = res[1]
        outs_p.append((
            z3[:, :, COL_C_K:COL_C_K + W_BR].reshape(bp, t_len, H_C, HD_C),
            z3[:, :, COL_C_V:COL_C_V + W_BR].reshape(bp, t_len, H_C, HD_C),
            hT[:bp].reshape(bp, G_A, P_A), hT[bp:].reshape(bp, G_A, P_A),
            _wkv_from_pairs(sT, bp),
            z3[:, -1, COL_D_R:COL_D_R + SHIFT_W],
        ))

        zs = _in_proj(xs, mod_s, norm_g[l], w_in_p[l], db)
        sa, hr, hi = _s5_sample(zs, state_ssm_re[l].reshape(db, S5_W), state_ssm_im[l].reshape(db, S5_W),
                                dense, abr, abi, ssm_d[l], w_glu_b[l], ssm_b_glu[l])
        sb, v_rows = _sgu_sample(zs, sgu_ln_g[l], sgu_ln_b[l], sgu_w[l], sgu_b[l])
        sc = _sb_sample(zs, ck, cv, page_table, sb_bias[l], l * n_pool)
        preps = _rwkv_prep(zs, state_shift[l], db, 1, *rw)
        sd, sTs = _rwkv_scan(preps, zs.reshape(1, db, IN_W_PAD), _wkv_to_pairs(state_wkv[l]),
                             rwkv_lnx_g[l], rwkv_lnx_b[l], True)
        res = _merge(xs, (sa, sb, sc, sd.reshape(db, W_BR)), zs, mod_s,
                     w_branch_b[l], w_out_b[l], final_norm_g, db, final)
        xs = res[0]
        if final:
            ys_out = res[1]
        outs_s.append((
            zs[:, COL_C_K:COL_C_K + W_BR].reshape(db, 1, H_C, HD_C),
            zs[:, COL_C_V:COL_C_V + W_BR].reshape(db, 1, H_C, HD_C),
            hr.reshape(db, G_A, P_A), hi.reshape(db, G_A, P_A),
            _wkv_from_pairs(sTs, db),
            zs[:, COL_D_R:COL_D_R + SHIFT_W],
            v_rows.reshape(db, 1, W_BR),
        ))

    stk = lambda outs, i: jnp.stack([o[i] for o in outs])
    return (yp.reshape(bp, t_len, D_MODEL), ys_out.reshape(db, 1, D_MODEL),
            stk(outs_p, 0), stk(outs_p, 1), stk(outs_s, 0), stk(outs_s, 1),
            stk(outs_p, 2), stk(outs_p, 3), stk(outs_s, 2), stk(outs_s, 3),
            stk(outs_p, 4), stk(outs_s, 4), stk(outs_p, 5), stk(outs_s, 5), stk(outs_s, 6))
```

```python
import functools
import math

import jax
import jax.numpy as jnp
import numpy as np
from jax import lax
from jax.experimental import pallas as pl
from jax.experimental.pallas import tpu as pltpu

F32 = jnp.float32
BF16 = jnp.bfloat16

D_MODEL = 1024
W_BR = D_MODEL // 2
N_BR = 4
CG_A = 16
G_A = W_BR // CG_A
P_A = 64
S5_W = G_A * P_A
S5_BLOCKS = 2
CHUNK = 128
HG_B = 4
H_C = 8
HD_C = W_BR // H_C
H_D = 8
N_D = W_BR // H_D
LR_W = 64
LR_A = 64
SHIFT_W = 3 * W_BR + LR_W + LR_A
GN_EPS = 64e-5
RMS_EPS = 1e-6
LN_EPS = 1e-5
LANE = 128
SUBLANE = 8

COL_A_U, COL_A_G = 0, 512
COL_B_U, COL_B_V, COL_B_G = 1024, 1536, 2048
COL_C_Q, COL_C_K, COL_C_V, COL_C_G = 2560, 3072, 3584, 4096
COL_D_R, COL_D_K, COL_D_V, COL_D_WA = 4608, 5120, 5632, 6144
COL_D_G = 6656
COL_M_G = 7168
IN_W_PAD = COL_M_G + N_BR * D_MODEL
IN_W_SRC_D_G = COL_D_WA + LR_W + LR_A

VMEM_LIMIT = 56 * 2**20


def _params(sem, vmem=VMEM_LIMIT):
    return pltpu.CompilerParams(dimension_semantics=sem, vmem_limit_bytes=vmem)


def _silu(x):
    return x * jax.nn.sigmoid(x)


def _gelu(x):
    c = math.sqrt(2.0 / math.pi)
    return 0.5 * x * (1.0 + jnp.tanh(c * (x + 0.044715 * (x * x * x))))


def _softplus(x):
    return jnp.maximum(x, 0.0) + jnp.log(1.0 + jnp.exp(-jnp.abs(x)))


def _bdot(a, b):
    return jnp.dot(a.astype(BF16), b.astype(BF16), preferred_element_type=F32)


def _split_dot(x, w, passes):
    pieces = []
    rem = x
    for p in range(passes):
        piece = rem.astype(BF16)
        pieces.append(piece)
        if p + 1 < passes:
            rem = rem - piece.astype(F32)
    if passes == 1:
        return jnp.dot(pieces[0], w, preferred_element_type=F32)
    return jnp.dot(jnp.concatenate(pieces, axis=1), jnp.concatenate([w] * passes, axis=0),
                   preferred_element_type=F32)


def _ada_kernel(c_ref, w_ref, b_ref, o_ref):
    o_ref[...] = _bdot(_silu(c_ref[...]), w_ref[...]) + b_ref[...]


def _ada_mod(c_all, w_ada, b_ada):
    depth = w_ada.shape[0]
    rows = c_all.shape[0]
    return pl.pallas_call(
        _ada_kernel,
        grid=(depth, 3),
        in_specs=[
            pl.BlockSpec((rows, D_MODEL), lambda l, k: (0, 0)),
            pl.BlockSpec((None, D_MODEL, D_MODEL), lambda l, k: (l, 0, k)),
            pl.BlockSpec((None, 1, D_MODEL), lambda l, k: (l, 0, k)),
        ],
        out_specs=pl.BlockSpec((None, rows, D_MODEL), lambda l, k: (l, 0, k)),
        out_shape=jax.ShapeDtypeStruct((depth, rows, 3 * D_MODEL), F32),
        compiler_params=_params(("arbitrary", "arbitrary")),
        name="ada_mod",
    )(c_all, w_ada, b_ada.reshape(depth, 1, 3 * D_MODEL))


def _inproj_kernel(x_ref, g_ref, sh_ref, sc_ref, w_ref, z_ref, h_scr):
    @pl.when(pl.program_id(1) == 0)
    def _():
        x = x_ref[...]
        ms = jnp.mean(x * x, axis=-1, keepdims=True)
        y = x * lax.rsqrt(ms + RMS_EPS) * g_ref[...]
        h_scr[...] = (y * (1.0 + sc_ref[...]) + sh_ref[...]).astype(BF16)

    z_ref[...] = jnp.dot(h_scr[...], w_ref[...], preferred_element_type=F32)


def _in_proj(x, mod, norm_g, w_in, layer, rows_per_group):
    n = x.shape[0]
    tm = min(1024, rows_per_group)
    tn = 1024
    r = mod.shape[1]
    bpg = rows_per_group // tm
    mod_spec = lambda k: pl.BlockSpec((None, r, D_MODEL), lambda i, j: (i // bpg, 0, k))
    return pl.pallas_call(
        _inproj_kernel,
        grid=(n // tm, IN_W_PAD // tn),
        in_specs=[
            pl.BlockSpec((tm, D_MODEL), lambda i, j: (i, 0)),
            pl.BlockSpec((1, D_MODEL), lambda i, j: (0, 0)),
            mod_spec(0),
            mod_spec(1),
            pl.BlockSpec((None, D_MODEL, tn), lambda i, j: (layer, 0, j)),
        ],
        out_specs=pl.BlockSpec((tm, tn), lambda i, j: (i, j)),
        out_shape=jax.ShapeDtypeStruct((n, IN_W_PAD), F32),
        scratch_shapes=[pltpu.VMEM((tm, D_MODEL), BF16)],
        compiler_params=_params(("arbitrary", "arbitrary")),
        name="in_proj",
    )(x, norm_g.reshape(1, D_MODEL), mod, mod, w_in)


def _s5_disc_kernel(ar_ref, ai_ref, ldt_ref, br_ref, bi_ref,
                    abr_ref, abi_ref, bbr_ref, bbi_ref):
    ar, ai = ar_ref[...], ai_ref[...]
    dt = jnp.exp(ldt_ref[...])
    mag = jnp.exp(ar * dt)
    abar_re, abar_im = mag * jnp.cos(ai * dt), mag * jnp.sin(ai * dt)
    den = ar * ar + ai * ai
    xr, xi = abar_re - 1.0, abar_im
    coef_re = (xr * ar + xi * ai) / den
    coef_im = (xi * ar - xr * ai) / den
    br, bi = br_ref[...], bi_ref[...]
    abr_ref[...] = abar_re
    abi_ref[...] = abar_im
    bbr_ref[...] = coef_re * br - coef_im * bi
    bbi_ref[...] = coef_re * bi + coef_im * br


def _s5_discretise(a_re, a_im, log_dt, b_re, b_im):
    gp = jax.ShapeDtypeStruct((G_A, 1, P_A), F32)
    gcp = jax.ShapeDtypeStruct((G_A, CG_A, P_A), F32)
    return pl.pallas_call(
        _s5_disc_kernel,
        out_shape=(gp, gp, gcp, gcp),
        name="s5_discretise",
    )(a_re.reshape(G_A, 1, P_A), a_im.reshape(G_A, 1, P_A), log_dt.reshape(G_A, 1, 1),
      jnp.swapaxes(b_re, 1, 2), jnp.swapaxes(b_im, 1, 2))


def _s5_dense_weights(abar_re, abar_im, bbar_re, bbar_im, c_re, c_im, nb):
    eye = jnp.eye(G_A, dtype=F32)
    bd = lambda m: jnp.einsum('gcp,gh->gchp', m, eye).reshape(W_BR, S5_W).astype(BF16)
    cd = lambda m: jnp.einsum('gcp,gh->gphc', m, eye).reshape(S5_W, W_BR).astype(BF16)
    ar = abar_re.reshape(1, S5_W)
    ai = abar_im.reshape(1, S5_W)
    a1 = jnp.broadcast_to(ar, (2 * nb, S5_W))
    a2 = jnp.concatenate([jnp.broadcast_to(-ai, (nb, S5_W)), jnp.broadcast_to(ai, (nb, S5_W))], axis=0)
    cdr, cdi = cd(c_re), cd(c_im)
    cw, sw = W_BR // S5_BLOCKS, S5_W // S5_BLOCKS
    ccat = jnp.stack([jnp.concatenate([cdr[f * sw:(f + 1) * sw, f * cw:(f + 1) * cw],
                                       -cdi[f * sw:(f + 1) * sw, f * cw:(f + 1) * cw]], axis=0)
                      for f in range(S5_BLOCKS)])
    return bd(bbar_re), bd(bbar_im), cdr, cdi, a1, a2, ccat


def _s5_tail(y, u, g, d_ref, wglu_ref, bglu_ref):
    y = y + d_ref[...] * u
    y = y * jax.nn.sigmoid(_bdot(_gelu(y), wglu_ref[...]) + bglu_ref[...])
    return y * _silu(g)


def _s5_prompt_kernel(u_ref, g_ref, perm_ref, unperm_ref, bdr_ref, bdi_ref, ccat_ref, a1_ref, a2_ref,
                      d_ref, wglu_ref, bglu_ref, y_ref, hT_ref, hs_scr, h_scr, *, nb, lc):
    c = pl.program_id(0)

    @pl.when(c == 0)
    def _():
        h_scr[...] = jnp.zeros_like(h_scr)

    rows = nb * lc
    n_cg = S5_W // LANE
    u = u_ref[...].reshape(rows, W_BR)
    ub = jnp.dot(perm_ref[...], u.astype(BF16), preferred_element_type=F32).astype(BF16)
    cw, sw = W_BR // S5_BLOCKS, S5_W // S5_BLOCKS
    for half, bd_ref in enumerate((bdr_ref, bdi_ref)):
        for f in range(S5_BLOCKS):
            bu = jnp.dot(ub[:, f * cw:(f + 1) * cw], bd_ref[f * cw:(f + 1) * cw, f * sw:(f + 1) * sw],
                         preferred_element_type=F32)
            for c in range(sw // LANE):
                cg = f * (sw // LANE) + c
                hs_scr[cg, half * rows:(half + 1) * rows, :] = bu[:, c * LANE:(c + 1) * LANE]
    a1 = [a1_ref[cg] for cg in range(n_cg)]
    a2 = [a2_ref[cg] for cg in range(n_cg)]

    top = lax.broadcasted_iota(jnp.int32, (2 * nb, LANE), 0) < nb

    def two_steps(tp, hs):
        r_re = pl.ds(pl.multiple_of(tp * 2 * nb, SUBLANE), 2 * nb)
        r_im = pl.ds(pl.multiple_of(rows + tp * 2 * nb, SUBLANE), 2 * nb)
        out = []
        for cg in range(n_cg):
            xr, xi = hs_scr[cg, r_re, :], hs_scr[cg, r_im, :]
            bu0 = jnp.where(top, xr, pltpu.roll(xi, nb, 0))
            bu1 = jnp.where(top, pltpu.roll(xr, nb, 0), xi)
            h1 = a1[cg] * hs[cg] + a2[cg] * pltpu.roll(hs[cg], nb, 0) + bu0
            h1s = pltpu.roll(h1, nb, 0)
            h2 = a1[cg] * h1 + a2[cg] * h1s + bu1
            hs_scr[cg, r_re, :] = jnp.where(top, h1, pltpu.roll(h2, nb, 0))
            hs_scr[cg, r_im, :] = jnp.where(top, h1s, h2)
            out.append(h2)
        return tuple(out)

    hs = lax.fori_loop(0, lc // 2, two_steps, tuple(h_scr[cg] for cg in range(n_cg)))
    for cg in range(n_cg):
        h_scr[cg] = hs[cg]
        hT_ref[cg] = hs[cg]
    ys = []
    for f in range(S5_BLOCKS):
        cgs = range(f * (sw // LANE), (f + 1) * (sw // LANE))
        h_re = jnp.concatenate([hs_scr[cg, 0:rows, :] for cg in cgs], axis=1).astype(BF16)
        h_im = jnp.concatenate([hs_scr[cg, rows:2 * rows, :] for cg in cgs], axis=1).astype(BF16)
        ys.append(jnp.dot(jnp.concatenate([h_re, h_im], axis=1), ccat_ref[f], preferred_element_type=F32))
    y_tb = jnp.concatenate(ys, axis=1)
    y_hi = y_tb.astype(BF16)
    y_lo = (y_tb - y_hi.astype(F32)).astype(BF16)
    y = jnp.dot(unperm_ref[...], jnp.concatenate([y_hi, y_lo], axis=0), preferred_element_type=F32)
    g = g_ref[...].reshape(rows, W_BR)
    out = _s5_tail(y, u, g, d_ref, wglu_ref, bglu_ref)
    y_ref[...] = out.reshape(nb, lc, W_BR).astype(y_ref.dtype)


def _s5_prompt(z3, dense, d, w_glu, b_glu):
    nb, t_len, _ = z3.shape
    assert 2 * nb == SUBLANE, "state rows [re; im] of all sequences must fill one sublane tile"
    lc = min(128, t_len)
    bdr, bdi, _, _, a1, a2, ccat = dense
    full = lambda shape: pl.BlockSpec(shape, lambda c: (0,) * len(shape))
    kern = functools.partial(_s5_prompt_kernel, nb=nb, lc=lc)
    n_cg = S5_W // LANE
    slabs = lambda a: a.reshape(2 * nb, n_cg, LANE).transpose(1, 0, 2)
    rows = nb * lc
    src = (np.arange(rows) % nb) * lc + np.arange(rows) // nb
    perm_np = np.zeros((rows, rows), np.float32)
    perm_np[np.arange(rows), src] = 1.0
    perm = jnp.asarray(perm_np, BF16)
    unperm = jnp.asarray(np.concatenate([perm_np.T, perm_np.T], axis=1), BF16)
    y, hT = pl.pallas_call(
        kern,
        grid=(t_len // lc,),
        in_specs=[
            pl.BlockSpec((nb, lc, W_BR), lambda c: (0, c, COL_A_U // W_BR)),
            pl.BlockSpec((nb, lc, W_BR), lambda c: (0, c, COL_A_G // W_BR)),
            full((rows, rows)), full((rows, 2 * rows)),
            full((W_BR, S5_W)), full((W_BR, S5_W)), full(ccat.shape),
            full((n_cg, 2 * nb, LANE)), full((n_cg, 2 * nb, LANE)),
            full((1, W_BR)), full((W_BR, W_BR)), full((1, W_BR)),
        ],
        out_specs=[
            pl.BlockSpec((nb, lc, W_BR), lambda c: (0, c, 0)),
            full((n_cg, 2 * nb, LANE)),
        ],
        out_shape=(jax.ShapeDtypeStruct((nb, t_len, W_BR), BF16),
                   jax.ShapeDtypeStruct((n_cg, 2 * nb, LANE), F32)),
        scratch_shapes=[pltpu.VMEM((n_cg, 2 * nb * lc, LANE), F32), pltpu.VMEM((n_cg, 2 * nb, LANE), F32)],
        compiler_params=_params(("arbitrary",)),
        name="s5_prompt",
    )(z3, z3, perm, unperm, bdr, bdi, ccat, slabs(a1), slabs(a2), d.reshape(1, W_BR), w_glu,
      b_glu.reshape(1, W_BR))
    return y, hT.transpose(1, 0, 2).reshape(2 * nb, S5_W)


def _s5_sample_kernel(u_ref, g_ref, h0r_ref, h0i_ref, bdr_ref, bdi_ref, cdr_ref, cdi_ref,
                      ar_ref, ai_ref, d_ref, wglu_ref, bglu_ref, y_ref, hr_ref, hi_ref):
    u = u_ref[...]
    ub = u.astype(BF16)
    ar, ai = ar_ref[...], ai_ref[...]
    h0r, h0i = h0r_ref[...], h0i_ref[...]
    hr = jnp.dot(ub, bdr_ref[...], preferred_element_type=F32) + (ar * h0r - ai * h0i)
    hi = jnp.dot(ub, bdi_ref[...], preferred_element_type=F32) + (ar * h0i + ai * h0r)
    hr_ref[...] = hr
    hi_ref[...] = hi
    y = _bdot(hr, cdr_ref[...]) - _bdot(hi, cdi_ref[...])
    y_ref[...] = _s5_tail(y, u, g_ref[...], d_ref, wglu_ref, bglu_ref).astype(y_ref.dtype)


def _s5_sample(z, h0_re, h0_im, dense, abar_re, abar_im, d, w_glu, b_glu):
    nb = z.shape[0]
    bdr, bdi, cdr, cdi = dense[:4]
    full = lambda shape: pl.BlockSpec(shape, lambda i: (0,) * len(shape))
    st = jax.ShapeDtypeStruct((nb, S5_W), F32)
    return pl.pallas_call(
        _s5_sample_kernel,
        grid=(1,),
        in_specs=[
            pl.BlockSpec((nb, W_BR), lambda i: (0, COL_A_U // W_BR)),
            pl.BlockSpec((nb, W_BR), lambda i: (0, COL_A_G // W_BR)),
            full((nb, S5_W)), full((nb, S5_W)),
            full((W_BR, S5_W)), full((W_BR, S5_W)), full((S5_W, W_BR)), full((S5_W, W_BR)),
            full((1, S5_W)), full((1, S5_W)),
            full((1, W_BR)), full((W_BR, W_BR)), full((1, W_BR)),
        ],
        out_specs=[full((nb, W_BR)), full((nb, S5_W)), full((nb, S5_W))],
        out_shape=(jax.ShapeDtypeStruct((nb, W_BR), BF16), st, st),
        compiler_params=_params(("arbitrary",)),
        name="s5_sample",
    )(z, z, h0_re, h0_im, bdr, bdi, cdr, cdi, abar_re.reshape(1, S5_W), abar_im.reshape(1, S5_W),
      d.reshape(1, W_BR), w_glu, b_glu.reshape(1, W_BR))


def _layernorm(v, g_ref, b_ref):
    mu = jnp.mean(v, axis=-1, keepdims=True)
    vc = v - mu
    var = jnp.mean(vc * vc, axis=-1, keepdims=True)
    return vc * lax.rsqrt(var + LN_EPS) * g_ref[...] + b_ref[...]


def _sgu_prompt_kernel(u_ref, v_ref, g_ref, lng_ref, lnb_ref, ws_ref, bs_ref, y_ref, *, n_chunks):
    cw = W_BR // HG_B
    vn = _layernorm(_gelu(v_ref[...]), lng_ref, lnb_ref)
    row = lax.broadcasted_iota(jnp.int32, (CHUNK, CHUNK), 0)
    col = lax.broadcasted_iota(jnp.int32, (CHUNK, CHUNK), 1)
    ws = [jnp.where(col <= row, ws_ref[h], 0.0).astype(BF16) for h in range(HG_B)]
    bs = bs_ref[...]
    for c in range(n_chunks):
        rows = slice(c * CHUNK, (c + 1) * CHUNK)
        vb = vn[rows, :].astype(BF16)
        mix = jnp.concatenate(
            [jnp.dot(ws[h], vb[:, h * cw:(h + 1) * cw], preferred_element_type=F32) for h in range(HG_B)],
            axis=1) + bs
        out = _gelu(u_ref[rows, :]) * mix * _silu(g_ref[rows, :])
        y_ref[rows, :] = out.astype(y_ref.dtype)


def _sgu_prompt(z, ln_g, ln_b, w_s, b_s):
    n = z.shape[0]
    tm = min(512, n)
    full = lambda shape: pl.BlockSpec(shape, lambda i: (0,) * len(shape))
    bs_full = jnp.repeat(b_s.T, W_BR // HG_B, axis=1)
    kern = functools.partial(_sgu_prompt_kernel, n_chunks=tm // CHUNK)
    col = lambda c: pl.BlockSpec((tm, W_BR), lambda i: (i, c // W_BR))
    return pl.pallas_call(
        kern,
        grid=(n // tm,),
        in_specs=[col(COL_B_U), col(COL_B_V), col(COL_B_G),
                  full((1, W_BR)), full((1, W_BR)), full((HG_B, CHUNK, CHUNK)), full((CHUNK, W_BR))],
        out_specs=pl.BlockSpec((tm, W_BR), lambda i: (i, 0)),
        out_shape=jax.ShapeDtypeStruct((n, W_BR), BF16),
        compiler_params=_params(("arbitrary",)),
        name="sgu_prompt",
    )(z, z, z, ln_g.reshape(1, W_BR), ln_b.reshape(1, W_BR), w_s, bs_full)


def _sgu_sample_kernel(u_ref, v_ref, g_ref, lng_ref, lnb_ref, w00_ref, b0_ref, y_ref, vn_ref):
    vn = _layernorm(_gelu(v_ref[...]), lng_ref, lnb_ref)
    vn_ref[...] = vn
    mix = vn * w00_ref[...] + b0_ref[...]
    y_ref[...] = (_gelu(u_ref[...]) * mix * _silu(g_ref[...])).astype(y_ref.dtype)


def _sgu_sample(z, ln_g, ln_b, w_s, b_s):
    nb = z.shape[0]
    cw = W_BR // HG_B
    full = lambda shape: pl.BlockSpec(shape, lambda i: (0,) * len(shape))
    col = lambda c: pl.BlockSpec((nb, W_BR), lambda i: (0, c // W_BR))
    w00 = jnp.repeat(w_s[:, 0, 0], cw).reshape(1, W_BR)
    b0 = jnp.repeat(b_s[:, 0], cw).reshape(1, W_BR)
    return pl.pallas_call(
        _sgu_sample_kernel,
        grid=(1,),
        in_specs=[col(COL_B_U), col(COL_B_V), col(COL_B_G),
                  full((1, W_BR)), full((1, W_BR)), full((1, W_BR)), full((1, W_BR))],
        out_specs=[full((nb, W_BR)), full((nb, W_BR))],
        out_shape=(jax.ShapeDtypeStruct((nb, W_BR), BF16), jax.ShapeDtypeStruct((nb, W_BR), F32)),
        compiler_params=_params(("arbitrary",)),
        name="sgu_sample",
    )(z, z, z, ln_g.reshape(1, W_BR), ln_b.reshape(1, W_BR), w00, b0)


def _suffix_matrix():
    j = np.arange(LANE)[:, None]
    s = np.arange(LANE)[None, :]
    return jnp.asarray(np.concatenate([(j >= s), np.ones((LANE, LANE), bool)], axis=1), dtype=BF16)


LOG2E = 1.0 / math.log(2.0)


def _softplus2(z):
    return jnp.maximum(z, 0.0) + jnp.log2(1.0 + jnp.exp2(-jnp.abs(z)))


def _sb_prompt_kernel(it_ref, jt_ref, bias_ref, q_ref, k_ref, v_ref, g_ref, m_ref, o_ref,
                      acc0, acc1, car0, car1, *, tq, tk):
    hp = pl.program_id(1)
    p = pl.program_id(2)
    i = it_ref[p]
    j = jt_ref[p]
    nsub = tk // LANE
    accs, cars = (acc0, acc1), (car0, car1)

    @pl.when(j == 0)
    def _():
        for r in (acc0, acc1, car0, car1):
            r[...] = jnp.zeros_like(r)

    def process(masked):
        lane = lax.broadcasted_iota(jnp.int32, (tq, LANE), 1)
        q = q_ref[...] * (HD_C ** -0.5 * LOG2E)
        qh = (jnp.where(lane < HD_C, q, 0.0).astype(BF16), jnp.where(lane >= HD_C, q, 0.0).astype(BF16))
        m = m_ref[...]
        kb = k_ref[...].astype(BF16)
        vb = v_ref[...].astype(BF16)
        if masked:
            valid = (lax.broadcasted_iota(jnp.int32, (tq, tk), 1)
                     < lax.broadcasted_iota(jnp.int32, (tq, tk), 0))
        for h in range(2):
            s = lax.dot_general(qh[h], kb, (((1,), (1,)), ((), ())), preferred_element_type=F32)
            z = s + bias_ref[2 * hp + h] * LOG2E
            sp = _softplus2(z)
            if masked:
                sp = jnp.where(valid, sp, 0.0)
            spb = sp.astype(BF16)
            rs = [jnp.dot(spb[:, sub * LANE:(sub + 1) * LANE], m, preferred_element_type=F32)
                  for sub in range(nsub)]
            c = cars[h][...]
            es = [None] * nsub
            for sub in reversed(range(nsub)):
                es[sub] = jnp.exp2(z[:, sub * LANE:(sub + 1) * LANE] - rs[sub][:, :LANE] - c)
                c = c + rs[sub][:, LANE:]
            cars[h][...] = c
            e = jnp.concatenate(es, axis=1)
            if masked:
                e = jnp.where(valid, e, 0.0)
            accs[h][...] += jnp.dot(e.astype(BF16), vb, preferred_element_type=F32)

    @pl.when(j == 0)
    def _():
        process(True)

    @pl.when(j > 0)
    def _():
        process(False)

    @pl.when(j == i)
    def _():
        lane = lax.broadcasted_iota(jnp.int32, (tq, LANE), 1)
        y = jnp.where(lane < HD_C, acc0[...], acc1[...])
        o_ref[...] = (y * _silu(g_ref[...])).astype(o_ref.dtype)


def _sb_prompt(z, sb_bias, nb, t_len):
    tq = tk = min(512, t_len)
    nq = t_len // tq
    pairs = [(i, j) for i in range(nq) for j in range(i + 1)]
    it = jnp.asarray([p[0] for p in pairs], jnp.int32)
    jt = jnp.asarray([p[1] for p in pairs], jnp.int32)
    kern = functools.partial(_sb_prompt_kernel, tq=tq, tk=tk)
    qspec = lambda c: pl.BlockSpec((tq, LANE), lambda b, h, p, it, jt: (b * nq + it[p], c // LANE + h))
    kspec = lambda c: pl.BlockSpec((tk, LANE), lambda b, h, p, it, jt: (b * nq + it[p] - jt[p], c // LANE + h))
    grid_spec = pltpu.PrefetchScalarGridSpec(
        num_scalar_prefetch=2,
        grid=(nb, H_C // 2, len(pairs)),
        in_specs=[
            pl.BlockSpec(memory_space=pltpu.SMEM),
            qspec(COL_C_Q), kspec(COL_C_K), kspec(COL_C_V), qspec(COL_C_G),
            pl.BlockSpec((LANE, 2 * LANE), lambda b, h, p, it, jt: (0, 0)),
        ],
        out_specs=pl.BlockSpec((tq, LANE), lambda b, h, p, it, jt: (b * nq + it[p], h)),
        scratch_shapes=[pltpu.VMEM((tq, LANE), F32)] * 4,
    )
    return pl.pallas_call(
        kern,
        grid_spec=grid_spec,
        out_shape=jax.ShapeDtypeStruct((nb * t_len, W_BR), BF16),
        compiler_params=_params(("arbitrary",) * 3),
        name="sb_prompt",
    )(it, jt, sb_bias, z, z, z, z, _suffix_matrix())


PAGES_PER_STEP = 8


def _sb_sample_kernel(pt_ref, qb_ref, g_ref, bias_ref, m_ref, *rest):
    k_refs = rest[:PAGES_PER_STEP]
    v_refs = rest[PAGES_PER_STEP:2 * PAGES_PER_STEP]
    o_ref, acc, car = rest[2 * PAGES_PER_STEP:]
    jj = pl.program_id(1)

    @pl.when(jj == 0)
    def _():
        acc[...] = jnp.zeros_like(acc)
        car[...] = jnp.zeros_like(car)

    qb = qb_ref[...] * (HD_C ** -0.5 * LOG2E)
    bias = bias_ref[...] * LOG2E
    m = m_ref[...]
    for r in reversed(range(PAGES_PER_STEP)):
        prod = k_refs[r][...] * qb
        z = jnp.concatenate([jnp.sum(prod[h * HD_C:(h + 1) * HD_C, :], axis=0, keepdims=True)
                             for h in range(H_C)], axis=0) + bias
        rr = _split_dot(_softplus2(z), m, 2)
        e = jnp.exp2(z - rr[:, :LANE] - car[...])
        car[...] += rr[:, LANE:]
        eb = jnp.concatenate([jnp.broadcast_to(e[h:h + 1, :], (HD_C, LANE)) for h in range(H_C)], axis=0)
        acc[...] += v_refs[r][...] * eb

    @pl.when(jj == pl.num_programs(1) - 1)
    def _():
        o_ref[...] = jnp.sum(acc[...], axis=1, keepdims=True) * _silu(g_ref[...])


def _sb_sample(z, cache_k, cache_v, page_table, sb_bias, base):
    nb = z.shape[0]
    n_pages = page_table.shape[1]
    page = cache_k.shape[2]
    assert page == LANE and n_pages % PAGES_PER_STEP == 0
    n_steps = n_pages // PAGES_PER_STEP
    bias = jnp.broadcast_to(sb_bias[:, None], (H_C, page))
    qb = jnp.broadcast_to(z[:, COL_C_Q:COL_C_Q + W_BR, None], (nb, W_BR, page))
    g3 = z[:, COL_C_G:COL_C_G + W_BR].reshape(nb, W_BR, 1)

    def page_spec(r):
        def imap(b, jj, pt):
            return (base + pt[b, (n_steps - 1 - jj) * PAGES_PER_STEP + r], 0, 0)
        return pl.BlockSpec((None, W_BR, page), imap)

    const = lambda shape: pl.BlockSpec(shape, lambda b, jj, pt: (0,) * len(shape))
    col_spec = pl.BlockSpec((None, W_BR, 1), lambda b, jj, pt: (b, 0, 0))
    grid_spec = pltpu.PrefetchScalarGridSpec(
        num_scalar_prefetch=1,
        grid=(nb, n_steps),
        in_specs=[pl.BlockSpec((None, W_BR, page), lambda b, jj, pt: (b, 0, 0)), col_spec,
                  const((H_C, page)), const((LANE, 2 * LANE))]
                 + [page_spec(r) for r in range(PAGES_PER_STEP)]
                 + [page_spec(r) for r in range(PAGES_PER_STEP)],
        out_specs=col_spec,
        scratch_shapes=[pltpu.VMEM((W_BR, page), F32), pltpu.VMEM((H_C, LANE), F32)],
    )
    out = pl.pallas_call(
        _sb_sample_kernel,
        grid_spec=grid_spec,
        out_shape=jax.ShapeDtypeStruct((nb, W_BR, 1), F32),
        compiler_params=_params(("arbitrary", "arbitrary")),
        name="sb_sample",
    )(page_table, qb, g3, bias, _suffix_matrix(),
      *([cache_k] * PAGES_PER_STEP), *([cache_v] * PAGES_PER_STEP))
    return out.reshape(nb, W_BR)


def _head_ones(width):
    idx = np.arange(width) // N_D
    return jnp.asarray(idx[:, None] == idx[None, :], dtype=BF16)


def _rwkv_prep_kernel(r_ref, k_ref, v_ref, wa_ref, pr_ref, pk_ref, pv_ref, pwa_ref,
                      mur_ref, muk_ref, muv_ref, muwa_ref, w0_ref, w2_ref, a0_ref, a2_ref,
                      kk_ref_, ka_ref, rk_ref, eh_ref,
                      ro_ref, ko_ref, vo_ref, dec_ref, kko_ref, bo_ref, bon_ref, *, roll_rows, first_zero):
    def shifted(x_ref, p_ref, mu_ref):
        x = x_ref[...]
        if roll_rows:
            row = lax.broadcasted_iota(jnp.int32, x.shape, 0)
            last = p_ref[SUBLANE - 1:SUBLANE, :]
            if first_zero:
                last = jnp.where(pl.program_id(1) == 0, 0.0, last)
            prev = jnp.where(row == 0, last, pltpu.roll(x, 1, 0))
        else:
            prev = p_ref[...]
        return x + (prev - x) * mu_ref[...]

    r = shifted(r_ref, pr_ref, mur_ref)
    k = shifted(k_ref, pk_ref, muk_ref)
    v = shifted(v_ref, pv_ref, muv_ref)
    wa = shifted(wa_ref, pwa_ref, muwa_ref)
    eh = eh_ref[...]
    w = -_softplus(-(w0_ref[...] + _bdot(jnp.tanh(wa), w2_ref[...]))) - 0.5
    a = jax.nn.sigmoid(a0_ref[...] + _bdot(wa, a2_ref[...]))
    kk = k * kk_ref_[...]
    kk = kk * lax.rsqrt(_split_dot(kk * kk, eh, 3) + 1e-12)
    k2 = k * (1.0 + (a - 1.0) * ka_ref[...])
    ro_ref[...] = r
    ko_ref[...] = k2
    vo_ref[...] = v
    dec_ref[...] = jnp.exp(-jnp.exp(w))
    kko_ref[...] = kk
    bo_ref[...] = kk * a
    bon_ref[...] = _split_dot(r * k2 * rk_ref[...], eh, 3) * v


def _rwkv_prep(z, prev, nb, t_len, mu, w0, w2, a0, a2, k_k, k_a, r_k):
    n = z.shape[0]
    tm = min(512, t_len) if prev is None else n
    bps = max(t_len // tm, 1)
    wa_w = LR_W + LR_A
    row = lambda v: v.reshape(1, -1)
    col = lambda c, w: pl.BlockSpec((tm, w), lambda b, i: (b * bps + i, c // w))
    if prev is None:
        tsub = tm // SUBLANE
        pcol = lambda c, w: pl.BlockSpec(
            (SUBLANE, w), lambda b, i: (jnp.maximum((b * bps + i) * tsub - 1, 0), c // w))
        prev_specs = [pcol(COL_D_R, W_BR), pcol(COL_D_K, W_BR), pcol(COL_D_V, W_BR), pcol(COL_D_WA, wa_w)]
        prev_args = [z, z, z, z]
        grid = (nb, bps)
    else:
        pfull = lambda w: pl.BlockSpec((tm, w), lambda b, i: (0, 0))
        prev_specs = [pfull(W_BR), pfull(W_BR), pfull(W_BR), pfull(wa_w)]
        prev_args = [prev[:, 0:W_BR], prev[:, W_BR:2 * W_BR], prev[:, 2 * W_BR:3 * W_BR], prev[:, 3 * W_BR:]]
        grid = (1, 1)
    full = lambda shape: pl.BlockSpec(shape, lambda b, i: (0,) * len(shape))
    w2p = jnp.concatenate([w2, jnp.zeros((LR_A, W_BR), F32)], axis=0).astype(BF16)
    a2p = jnp.concatenate([jnp.zeros((LR_W, W_BR), F32), a2], axis=0).astype(BF16)
    kern = functools.partial(_rwkv_prep_kernel, roll_rows=prev is None, first_zero=prev is None)
    out = jax.ShapeDtypeStruct((n, W_BR), F32)
    ospec = pl.BlockSpec((tm, W_BR), lambda b, i: (b * bps + i, 0))
    return pl.pallas_call(
        kern,
        grid=grid,
        in_specs=[col(COL_D_R, W_BR), col(COL_D_K, W_BR), col(COL_D_V, W_BR), col(COL_D_WA, wa_w)]
                 + prev_specs
                 + [full((1, W_BR))] * 3 + [full((1, wa_w))]
                 + [full((1, W_BR)), full((wa_w, W_BR)), full((1, W_BR)), full((wa_w, W_BR))]
                 + [full((1, W_BR))] * 3 + [full((W_BR, W_BR))],
        out_specs=[ospec] * 7,
        out_shape=(out,) * 7,
        compiler_params=_params(("arbitrary", "arbitrary")),
        name="rwkv_prep",
    )(z, z, z, z, *prev_args,
      row(mu[0:W_BR]), row(mu[W_BR:2 * W_BR]), row(mu[2 * W_BR:3 * W_BR]), row(mu[3 * W_BR:]),
      row(w0), w2p, row(a0), a2p, row(k_k), row(k_a), row(r_k), _head_ones(W_BR))


def _rwkv_scan_kernel(r_ref, k_ref, v_ref, dec_ref, kk_ref, b_ref, bon_ref, g_ref, s0_ref,
                      lng_ref, lnb_ref, e4_ref, eh_ref, y_ref, sT_ref, s_scr, y_scr,
                      *, nb, lc, state_per_row):
    n_hp = H_D // 2
    c = pl.program_id(0)

    @pl.when(c == 0)
    def _():
        s_scr[...] = s0_ref[...]

    e4 = e4_ref[...]
    sub = lax.broadcasted_iota(jnp.int32, (N_D, LANE), 0)
    lane = lax.broadcasted_iota(jnp.int32, (N_D, LANE), 1)
    diag = (lane % N_D) == sub
    pack = 2 * SUBLANE
    diag_b = diag.astype(F32).astype(BF16).reshape(N_D // pack, pack, LANE)
    combos = [(b, hp) for b in range(nb) for hp in range(n_hp)]
    nc = len(combos)
    assert nc % 2 == 0

    def split2(x):
        hi = x.astype(BF16)
        return jnp.concatenate([hi, (x - hi.astype(F32)).astype(BF16)], axis=1)

    def pairs(slabs):
        return [jnp.concatenate([slabs[q], slabs[q + 1]], axis=1) for q in range(0, len(slabs), 2)]

    def block(out, first_row, n):
        r = first_row + (n // 2) * N_D
        return out[r:r + N_D, (n % 2) * LANE:(n % 2 + 1) * LANE]

    def group(gi, carry):
        r0 = pl.multiple_of(gi * SUBLANE, SUBLANE)
        rows8 = pl.ds(r0, SUBLANE)
        load = lambda ref: [ref[b, rows8, :] for b in range(nb)]
        rr, kx, vx, dx, kkx, bx = (load(x) for x in (r_ref, k_ref, v_ref, dec_ref, kk_ref, b_ref))
        row = lambda blocks, b, hp, tt: blocks[b][tt:tt + 1, hp * LANE:(hp + 1) * LANE]
        state = lambda tt: [s_scr[((r0 + tt) if state_per_row else b) * n_hp + hp] for (b, hp) in combos]
        kk_rows = lambda s, tt: [split2(s[c] * (-row(kkx, b, hp, tt))) for c, (b, hp) in enumerate(combos)]

        def vdiag(tt, b, hp):
            piece = jnp.broadcast_to(row(vx, b, hp, tt), (pack, LANE)).astype(BF16)
            return (diag_b * piece[None]).reshape(N_D, LANE)

        s = state(0)
        out = jnp.dot(jnp.concatenate(
            kk_rows(s, 0) + pairs([vdiag(tt, b, hp) for tt in range(SUBLANE) for (b, hp) in combos]), axis=0),
            e4, preferred_element_type=F32)
        sa = [out[c * N_D:(c + 1) * N_D, :LANE] + out[c * N_D:(c + 1) * N_D, LANE:] for c in range(nc)]
        vb = [block(out, nc * N_D, n) for n in range(SUBLANE * nc)]

        y_rows = [[[None] * n_hp for _ in range(SUBLANE)] for _ in range(nb)]
        for tt in range(SUBLANE):
            last = tt == SUBLANE - 1
            new = []
            for c, (b, hp) in enumerate(combos):
                sn = (s[c] * row(dx, b, hp, tt) + sa[c] * row(bx, b, hp, tt)
                      + vb[tt * nc + c] * row(kx, b, hp, tt))
                if state_per_row or last:
                    s_scr[((r0 + tt) if state_per_row else b) * n_hp + hp] = sn
                new.append(sn)
            y_lhs = pairs([(new[c] * row(rr, b, hp, tt)).astype(BF16) for c, (b, hp) in enumerate(combos)])
            if last:
                lhs, y0 = y_lhs, 0
            else:
                s = state(tt + 1) if state_per_row else new
                lhs, y0 = kk_rows(s, tt + 1) + y_lhs, nc * N_D
            out = jnp.dot(jnp.concatenate(lhs, axis=0), e4, preferred_element_type=F32)
            if not last:
                sa = [out[c * N_D:(c + 1) * N_D, :LANE] + out[c * N_D:(c + 1) * N_D, LANE:] for c in range(nc)]
            for c, (b, hp) in enumerate(combos):
                y_rows[b][tt][hp] = jnp.sum(jnp.where(diag, block(out, y0, c), 0.0), axis=0, keepdims=True)
        for b in range(nb):
            y_scr[b, rows8, :] = jnp.concatenate(
                [jnp.concatenate(y_rows[b][tt], axis=1) for tt in range(SUBLANE)], axis=0)
        return carry

    lax.fori_loop(0, lc // SUBLANE, group, 0)

    sT_ref[...] = s_scr[...]
    eh = eh_ref[...]
    for b in range(nb):
        y = y_scr[b]
        ym = _split_dot(y, eh, 3) * (1.0 / N_D)
        yc = y - ym
        yv = _split_dot(yc * yc, eh, 3) * (1.0 / N_D)
        out = yc * lax.rsqrt(yv + GN_EPS) * lng_ref[...] + lnb_ref[...] + bon_ref[b]
        y_ref[b] = (out * _silu(g_ref[b])).astype(y_ref.dtype)


def _rwkv_scan(prep, z3, s0, lnx_g, lnx_b, state_per_row):
    nb, t_len, _ = z3.shape
    lc = t_len if state_per_row else min(128, t_len)
    p3 = [p.reshape(nb, t_len, W_BR) for p in prep]
    blk = pl.BlockSpec((nb, lc, W_BR), lambda c: (0, c, 0))
    full = lambda shape: pl.BlockSpec(shape, lambda c: (0,) * len(shape))
    kern = functools.partial(_rwkv_scan_kernel, nb=nb, lc=lc, state_per_row=state_per_row)
    n_st = s0.shape[0]
    return pl.pallas_call(
        kern,
        grid=(t_len // lc,),
        in_specs=[blk] * 7
                 + [pl.BlockSpec((nb, lc, W_BR), lambda c: (0, c, COL_D_G // W_BR)),
                    full((n_st, N_D, LANE)), full((1, W_BR)), full((1, W_BR)),
                    full((2 * LANE, 2 * LANE)), full((W_BR, W_BR))],
        out_specs=[blk, full((n_st, N_D, LANE))],
        out_shape=(jax.ShapeDtypeStruct((nb, t_len, W_BR), BF16),
                   jax.ShapeDtypeStruct((n_st, N_D, LANE), F32)),
        scratch_shapes=[pltpu.VMEM((n_st, N_D, LANE), F32), pltpu.VMEM((nb, lc, W_BR), F32)],
        compiler_params=_params(("arbitrary",)),
        name="rwkv_scan",
    )(p3[0], p3[1], p3[2], p3[3], p3[4], p3[5], p3[6], z3, s0,
      lnx_g.reshape(1, W_BR), lnx_b.reshape(1, W_BR), _head_ones(2 * LANE), _head_ones(W_BR))


def _wkv_to_pairs(s):
    nb = s.shape[0]
    return s.reshape(nb, H_D // 2, 2, N_D, N_D).transpose(0, 1, 3, 2, 4).reshape(nb * H_D // 2, N_D, LANE)


def _wkv_from_pairs(s, nb):
    return s.reshape(nb, H_D // 2, N_D, 2, N_D).transpose(0, 1, 3, 2, 4).reshape(nb, H_D, N_D, N_D)


def _merge_kernel(x_ref, ya_ref, yb_ref, yc_ref, yd_ref, mga_ref, mgb_ref, mgc_ref, mgd_ref,
                  gt_ref, wb_ref, wo_ref, fg_ref, *out_refs, final):
    merged = None
    branches = ((ya_ref, mga_ref), (yb_ref, mgb_ref), (yc_ref, mgc_ref), (yd_ref, mgd_ref))
    for n, (y_ref, mg_ref) in enumerate(branches):
        proj = jnp.dot(y_ref[...].astype(BF16), wb_ref[n], preferred_element_type=F32)
        term = jax.nn.sigmoid(mg_ref[...]) * proj
        merged = term if merged is None else merged + term
    x_new = x_ref[...] + gt_ref[...] * _bdot(merged, wo_ref[...])
    out_refs[0][...] = x_new
    if final:
        ms = jnp.mean(x_new * x_new, axis=-1, keepdims=True)
        out_refs[1][...] = x_new * lax.rsqrt(ms + RMS_EPS) * fg_ref[...]


def _merge(x, ys, z, mod, w_branch, w_out, layer, final_g, rows_per_group, final):
    n = x.shape[0]
    tm = min(256, n)
    r = mod.shape[1]
    bpg = rows_per_group // tm
    full = lambda shape: pl.BlockSpec(shape, lambda i: (0,) * len(shape))
    yspec = pl.BlockSpec((tm, W_BR), lambda i: (i, 0))
    xspec = pl.BlockSpec((tm, D_MODEL), lambda i: (i, 0))
    out_sds = jax.ShapeDtypeStruct((n, D_MODEL), F32)
    outs = pl.pallas_call(
        functools.partial(_merge_kernel, final=final),
        grid=(n // tm,),
        in_specs=[xspec, yspec, yspec, yspec, yspec]
                 + [pl.BlockSpec((tm, D_MODEL), lambda i, n=n: (i, COL_M_G // D_MODEL + n)) for n in range(N_BR)]
                 + [pl.BlockSpec((None, r, D_MODEL), lambda i: (i // bpg, 0, 2)),
                  pl.BlockSpec((None, N_BR, W_BR, D_MODEL), lambda i: (layer, 0, 0, 0)),
                  pl.BlockSpec((None, D_MODEL, D_MODEL), lambda i: (layer, 0, 0)), full((1, D_MODEL))],
        out_specs=[xspec, xspec] if final else [xspec],
        out_shape=(out_sds, out_sds) if final else (out_sds,),
        compiler_params=_params(("arbitrary",)),
        name="merge",
    )(x, *ys, z, z, z, z, mod, w_branch, w_out, final_g.reshape(1, D_MODEL))
    return outs


def _pad_w_in(w_in):
    depth = w_in.shape[0]
    pad = jnp.zeros((depth, D_MODEL, COL_D_G - IN_W_SRC_D_G), w_in.dtype)
    return jnp.concatenate([w_in[:, :, :IN_W_SRC_D_G], pad, w_in[:, :, IN_W_SRC_D_G:]], axis=2).astype(BF16)


def kernel(x_prompt, x_sample, cache_k, cache_v, state_ssm_re, state_ssm_im, state_wkv, state_shift, page_table, c_prompt, c_sample, norm_g, w_ada, b_ada, w_in, ssm_a_re, ssm_a_im, ssm_log_dt, ssm_b_re, ssm_b_im, ssm_c_re, ssm_c_im, ssm_d, ssm_w_glu, ssm_b_glu, sgu_ln_g, sgu_ln_b, sgu_w, sgu_b, sb_bias, rwkv_mu, rwkv_w0, rwkv_w2, rwkv_a0, rwkv_a2, rwkv_k_k, rwkv_k_a, rwkv_r_k, rwkv_lnx_g, rwkv_lnx_b, w_branch, w_out, final_norm_g):
    bp, t_len, _ = x_prompt.shape
    db = x_sample.shape[0]
    depth = w_in.shape[0]
    n_pool, page = cache_k.shape[1], cache_k.shape[2]
    np_rows = bp * t_len

    n_c = bp + db
    c_rows = -(-n_c // SUBLANE) * SUBLANE
    c_all = jnp.concatenate([c_prompt, c_sample, jnp.zeros((c_rows - n_c, D_MODEL), F32)], axis=0)
    mod = _ada_mod(c_all, w_ada, b_ada)
    w_in_p = _pad_w_in(w_in)
    w_branch_b = w_branch.astype(BF16)
    w_out_b = w_out.astype(BF16)
    w_glu_b = ssm_w_glu.astype(BF16)
    ck = cache_k.transpose(0, 1, 3, 4, 2).reshape(depth * n_pool, W_BR, page)
    cv = cache_v.transpose(0, 1, 3, 4, 2).reshape(depth * n_pool, W_BR, page)

    xp = x_prompt.reshape(np_rows, D_MODEL)
    xs = x_sample.reshape(db, D_MODEL)
    zero_wkv = jnp.zeros((bp * H_D // 2, N_D, LANE), F32)
    outs_p, outs_s = [], []
    yp = ys_out = None
    for l in range(depth):
        final = l == depth - 1
        mod_p = mod[l, :bp].reshape(bp, 1, 3 * D_MODEL)
        mod_s = mod[l, bp:bp + db].reshape(1, db, 3 * D_MODEL)
        abr, abi, bbr, bbi = _s5_discretise(ssm_a_re[l], ssm_a_im[l], ssm_log_dt[l], ssm_b_re[l], ssm_b_im[l])
        dense = _s5_dense_weights(abr, abi, bbr, bbi, ssm_c_re[l], ssm_c_im[l], bp)
        rw = (rwkv_mu[l], rwkv_w0[l], rwkv_w2[l], rwkv_a0[l], rwkv_a2[l], rwkv_k_k[l], rwkv_k_a[l],
              rwkv_r_k[l].reshape(W_BR))

        z = _in_proj(xp, mod_p, norm_g[l], w_in_p, l, t_len)
        z3 = z.reshape(bp, t_len, IN_W_PAD)
        ya, hT = _s5_prompt(z3, dense, ssm_d[l], w_glu_b[l], ssm_b_glu[l])
        yb = _sgu_prompt(z, sgu_ln_g[l], sgu_ln_b[l], sgu_w[l], sgu_b[l])
        yc = _sb_prompt(z, sb_bias[l], bp, t_len)
        prep = _rwkv_prep(z, None, bp, t_len, *rw)
        yd, sT = _rwkv_scan(prep, z3, zero_wkv, rwkv_lnx_g[l], rwkv_lnx_b[l], False)
        res = _merge(xp, (ya.reshape(np_rows, W_BR), yb, yc, yd.reshape(np_rows, W_BR)), z, mod_p,
                     w_branch_b, w_out_b, l, final_norm_g, t_len, final)
        xp = res[0]
        if final:
            yp = res[1]
        outs_p.append((
            z3[:, :, COL_C_K:COL_C_K + W_BR].reshape(bp, t_len, H_C, HD_C),
            z3[:, :, COL_C_V:COL_C_V + W_BR].reshape(bp, t_len, H_C, HD_C),
            hT[:bp].reshape(bp, G_A, P_A), hT[bp:].reshape(bp, G_A, P_A),
            _wkv_from_pairs(sT, bp),
            z3[:, -1, COL_D_R:COL_D_R + SHIFT_W],
        ))

        zs = _in_proj(xs, mod_s, norm_g[l], w_in_p, l, db)
        sa, hr, hi = _s5_sample(zs, state_ssm_re[l].reshape(db, S5_W), state_ssm_im[l].reshape(db, S5_W),
                                dense, abr, abi, ssm_d[l], w_glu_b[l], ssm_b_glu[l])
        sb, v_rows = _sgu_sample(zs, sgu_ln_g[l], sgu_ln_b[l], sgu_w[l], sgu_b[l])
        sc = _sb_sample(zs, ck, cv, page_table, sb_bias[l], l * n_pool)
        preps = _rwkv_prep(zs, state_shift[l], db, 1, *rw)
        sd, sTs = _rwkv_scan(preps, zs.reshape(1, db, IN_W_PAD), _wkv_to_pairs(state_wkv[l]),
                             rwkv_lnx_g[l], rwkv_lnx_b[l], True)
        res = _merge(xs, (sa, sb, sc, sd.reshape(db, W_BR)), zs, mod_s,
                     w_branch_b, w_out_b, l, final_norm_g, db, final)
        xs = res[0]
        if final:
            ys_out = res[1]
        outs_s.append((
            zs[:, COL_C_K:COL_C_K + W_BR].reshape(db, 1, H_C, HD_C),
            zs[:, COL_C_V:COL_C_V + W_BR].reshape(db, 1, H_C, HD_C),
            hr.reshape(db, G_A, P_A), hi.reshape(db, G_A, P_A),
            _wkv_from_pairs(sTs, db),
            zs[:, COL_D_R:COL_D_R + SHIFT_W],
            v_rows.reshape(db, 1, W_BR),
        ))

    stk = lambda outs, i: jnp.stack([o[i] for o in outs])
    return (yp.reshape(bp, t_len, D_MODEL), ys_out.reshape(db, 1, D_MODEL),
            stk(outs_p, 0), stk(outs_p, 1), stk(outs_s, 0), stk(outs_s, 1),
            stk(outs_p, 2), stk(outs_p, 3), stk(outs_s, 2), stk(outs_s, 3),
            stk(outs_p, 4), stk(outs_s, 4), stk(outs_p, 5), stk(outs_s, 5), stk(outs_s, 6))
```

```python
import functools
import math

import jax
import jax.numpy as jnp
import numpy as np
from jax import lax
from jax.experimental import pallas as pl
from jax.experimental.pallas import tpu as pltpu

F32 = jnp.float32
BF16 = jnp.bfloat16

D_MODEL = 1024
W_BR = D_MODEL // 2
N_BR = 4
CG_A = 16
G_A = W_BR // CG_A
P_A = 64
S5_W = G_A * P_A
S5_BLOCKS = 2
CHUNK = 128
HG_B = 4
H_C = 8
HD_C = W_BR // H_C
H_D = 8
N_D = W_BR // H_D
LR_W = 64
LR_A = 64
SHIFT_W = 3 * W_BR + LR_W + LR_A
GN_EPS = 64e-5
RMS_EPS = 1e-6
LN_EPS = 1e-5
LANE = 128
SUBLANE = 8

COL_A_U, COL_A_G = 0, 512
COL_B_U, COL_B_V, COL_B_G = 1024, 1536, 2048
COL_C_Q, COL_C_K, COL_C_V, COL_C_G = 2560, 3072, 3584, 4096
COL_D_R, COL_D_K, COL_D_V, COL_D_WA = 4608, 5120, 5632, 6144
COL_D_G = 6656
COL_M_G = 7168
IN_W_PAD = COL_M_G + N_BR * D_MODEL
IN_W_SRC_D_G = COL_D_WA + LR_W + LR_A

VMEM_LIMIT = 56 * 2**20


def _params(sem, vmem=VMEM_LIMIT):
    return pltpu.CompilerParams(dimension_semantics=sem, vmem_limit_bytes=vmem)


def _silu(x):
    return x * jax.nn.sigmoid(x)


def _gelu(x):
    c = math.sqrt(2.0 / math.pi)
    return 0.5 * x * (1.0 + jnp.tanh(c * (x + 0.044715 * (x * x * x))))


def _softplus(x):
    return jnp.maximum(x, 0.0) + jnp.log(1.0 + jnp.exp(-jnp.abs(x)))


def _bdot(a, b):
    return jnp.dot(a.astype(BF16), b.astype(BF16), preferred_element_type=F32)


def _split_dot(x, w, passes):
    pieces = []
    rem = x
    for p in range(passes):
        piece = rem.astype(BF16)
        pieces.append(piece)
        if p + 1 < passes:
            rem = rem - piece.astype(F32)
    if passes == 1:
        return jnp.dot(pieces[0], w, preferred_element_type=F32)
    return jnp.dot(jnp.concatenate(pieces, axis=1), jnp.concatenate([w] * passes, axis=0),
                   preferred_element_type=F32)


def _ada_kernel(c_ref, w_ref, b_ref, o_ref):
    o_ref[...] = _bdot(_silu(c_ref[...]), w_ref[...]) + b_ref[...]


def _ada_mod(c_all, w_ada, b_ada):
    depth = w_ada.shape[0]
    rows = c_all.shape[0]
    return pl.pallas_call(
        _ada_kernel,
        grid=(depth, 3),
        in_specs=[
            pl.BlockSpec((rows, D_MODEL), lambda l, k: (0, 0)),
            pl.BlockSpec((None, D_MODEL, D_MODEL), lambda l, k: (l, 0, k)),
            pl.BlockSpec((None, 1, D_MODEL), lambda l, k: (l, 0, k)),
        ],
        out_specs=pl.BlockSpec((None, rows, D_MODEL), lambda l, k: (l, 0, k)),
        out_shape=jax.ShapeDtypeStruct((depth, rows, 3 * D_MODEL), F32),
        compiler_params=_params(("arbitrary", "arbitrary")),
        name="ada_mod",
    )(c_all, w_ada, b_ada.reshape(depth, 1, 3 * D_MODEL))


def _inproj_kernel(x_ref, g_ref, sh_ref, sc_ref, w_ref, z_ref, h_scr):
    @pl.when(pl.program_id(1) == 0)
    def _():
        x = x_ref[...]
        ms = jnp.mean(x * x, axis=-1, keepdims=True)
        y = x * lax.rsqrt(ms + RMS_EPS) * g_ref[...]
        h_scr[...] = (y * (1.0 + sc_ref[...]) + sh_ref[...]).astype(BF16)

    z_ref[...] = jnp.dot(h_scr[...], w_ref[...], preferred_element_type=F32)


def _in_proj(x, mod, norm_g, w_in, layer, rows_per_group):
    n = x.shape[0]
    tm = min(2048, rows_per_group)
    tn = 1024
    r = mod.shape[1]
    bpg = rows_per_group // tm
    mod_spec = lambda k: pl.BlockSpec((None, r, D_MODEL), lambda i, j: (i // bpg, 0, k))
    return pl.pallas_call(
        _inproj_kernel,
        grid=(n // tm, IN_W_PAD // tn),
        in_specs=[
            pl.BlockSpec((tm, D_MODEL), lambda i, j: (i, 0)),
            pl.BlockSpec((1, D_MODEL), lambda i, j: (0, 0)),
            mod_spec(0),
            mod_spec(1),
            pl.BlockSpec((None, D_MODEL, tn), lambda i, j: (layer, 0, j)),
        ],
        out_specs=pl.BlockSpec((tm, tn), lambda i, j: (i, j)),
        out_shape=jax.ShapeDtypeStruct((n, IN_W_PAD), F32),
        scratch_shapes=[pltpu.VMEM((tm, D_MODEL), BF16)],
        compiler_params=_params(("arbitrary", "arbitrary")),
        name="in_proj",
    )(x, norm_g.reshape(1, D_MODEL), mod, mod, w_in)


def _s5_disc_kernel(ar_ref, ai_ref, ldt_ref, br_ref, bi_ref,
                    abr_ref, abi_ref, bbr_ref, bbi_ref):
    ar, ai = ar_ref[...], ai_ref[...]
    dt = jnp.exp(ldt_ref[...])
    mag = jnp.exp(ar * dt)
    abar_re, abar_im = mag * jnp.cos(ai * dt), mag * jnp.sin(ai * dt)
    den = ar * ar + ai * ai
    xr, xi = abar_re - 1.0, abar_im
    coef_re = (xr * ar + xi * ai) / den
    coef_im = (xi * ar - xr * ai) / den
    br, bi = br_ref[...], bi_ref[...]
    abr_ref[...] = abar_re
    abi_ref[...] = abar_im
    bbr_ref[...] = coef_re * br - coef_im * bi
    bbi_ref[...] = coef_re * bi + coef_im * br


def _s5_discretise(a_re, a_im, log_dt, b_re, b_im):
    gp = jax.ShapeDtypeStruct((G_A, 1, P_A), F32)
    gcp = jax.ShapeDtypeStruct((G_A, CG_A, P_A), F32)
    return pl.pallas_call(
        _s5_disc_kernel,
        out_shape=(gp, gp, gcp, gcp),
        name="s5_discretise",
    )(a_re.reshape(G_A, 1, P_A), a_im.reshape(G_A, 1, P_A), log_dt.reshape(G_A, 1, 1),
      jnp.swapaxes(b_re, 1, 2), jnp.swapaxes(b_im, 1, 2))


def _s5_dense_weights(abar_re, abar_im, bbar_re, bbar_im, c_re, c_im, nb):
    eye = jnp.eye(G_A, dtype=F32)
    bd = lambda m: jnp.einsum('gcp,gh->gchp', m, eye).reshape(W_BR, S5_W).astype(BF16)
    cd = lambda m: jnp.einsum('gcp,gh->gphc', m, eye).reshape(S5_W, W_BR).astype(BF16)
    ar = abar_re.reshape(1, S5_W)
    ai = abar_im.reshape(1, S5_W)
    a1 = jnp.broadcast_to(ar, (2 * nb, S5_W))
    a2 = jnp.concatenate([jnp.broadcast_to(-ai, (nb, S5_W)), jnp.broadcast_to(ai, (nb, S5_W))], axis=0)
    cdr, cdi = cd(c_re), cd(c_im)
    cw, sw = W_BR // S5_BLOCKS, S5_W // S5_BLOCKS
    ccat = jnp.stack([jnp.concatenate([cdr[f * sw:(f + 1) * sw, f * cw:(f + 1) * cw],
                                       -cdi[f * sw:(f + 1) * sw, f * cw:(f + 1) * cw]], axis=0)
                      for f in range(S5_BLOCKS)])
    return bd(bbar_re), bd(bbar_im), cdr, cdi, a1, a2, ccat


def _s5_tail(y, u, g, d_ref, wglu_ref, bglu_ref):
    y = y + d_ref[...] * u
    y = y * jax.nn.sigmoid(_bdot(_gelu(y), wglu_ref[...]) + bglu_ref[...])
    return y * _silu(g)


def _s5_prompt_kernel(u_ref, g_ref, perm_ref, unperm_ref, bdr_ref, bdi_ref, ccat_ref, a1_ref, a2_ref,
                      d_ref, wglu_ref, bglu_ref, y_ref, hT_ref, hs_scr, h_scr, *, nb, lc):
    c = pl.program_id(0)

    @pl.when(c == 0)
    def _():
        h_scr[...] = jnp.zeros_like(h_scr)

    rows = nb * lc
    n_cg = S5_W // LANE
    u = u_ref[...].reshape(rows, W_BR)
    ub = jnp.dot(perm_ref[...], u.astype(BF16), preferred_element_type=F32).astype(BF16)
    cw, sw = W_BR // S5_BLOCKS, S5_W // S5_BLOCKS
    for half, bd_ref in enumerate((bdr_ref, bdi_ref)):
        for f in range(S5_BLOCKS):
            bu = jnp.dot(ub[:, f * cw:(f + 1) * cw], bd_ref[f * cw:(f + 1) * cw, f * sw:(f + 1) * sw],
                         preferred_element_type=F32)
            for c in range(sw // LANE):
                cg = f * (sw // LANE) + c
                hs_scr[cg, half * rows:(half + 1) * rows, :] = bu[:, c * LANE:(c + 1) * LANE]
    a1 = [a1_ref[cg] for cg in range(n_cg)]
    a2 = [a2_ref[cg] for cg in range(n_cg)]

    top = lax.broadcasted_iota(jnp.int32, (2 * nb, LANE), 0) < nb

    def two_steps(tp, hs):
        r_re = pl.ds(pl.multiple_of(tp * 2 * nb, SUBLANE), 2 * nb)
        r_im = pl.ds(pl.multiple_of(rows + tp * 2 * nb, SUBLANE), 2 * nb)
        out = []
        for cg in range(n_cg):
            xr, xi = hs_scr[cg, r_re, :], hs_scr[cg, r_im, :]
            bu0 = jnp.where(top, xr, pltpu.roll(xi, nb, 0))
            bu1 = jnp.where(top, pltpu.roll(xr, nb, 0), xi)
            h1 = a1[cg] * hs[cg] + a2[cg] * pltpu.roll(hs[cg], nb, 0) + bu0
            h1s = pltpu.roll(h1, nb, 0)
            h2 = a1[cg] * h1 + a2[cg] * h1s + bu1
            hs_scr[cg, r_re, :] = jnp.where(top, h1, pltpu.roll(h2, nb, 0))
            hs_scr[cg, r_im, :] = jnp.where(top, h1s, h2)
            out.append(h2)
        return tuple(out)

    hs = lax.fori_loop(0, lc // 2, two_steps, tuple(h_scr[cg] for cg in range(n_cg)))
    for cg in range(n_cg):
        h_scr[cg] = hs[cg]
        hT_ref[cg] = hs[cg]
    ys = []
    for f in range(S5_BLOCKS):
        cgs = range(f * (sw // LANE), (f + 1) * (sw // LANE))
        h_re = jnp.concatenate([hs_scr[cg, 0:rows, :] for cg in cgs], axis=1).astype(BF16)
        h_im = jnp.concatenate([hs_scr[cg, rows:2 * rows, :] for cg in cgs], axis=1).astype(BF16)
        ys.append(jnp.dot(jnp.concatenate([h_re, h_im], axis=1), ccat_ref[f], preferred_element_type=F32))
    y_tb = jnp.concatenate(ys, axis=1)
    y_hi = y_tb.astype(BF16)
    y_lo = (y_tb - y_hi.astype(F32)).astype(BF16)
    y = jnp.dot(unperm_ref[...], jnp.concatenate([y_hi, y_lo], axis=0), preferred_element_type=F32)
    g = g_ref[...].reshape(rows, W_BR)
    out = _s5_tail(y, u, g, d_ref, wglu_ref, bglu_ref)
    y_ref[...] = out.reshape(nb, lc, W_BR).astype(y_ref.dtype)


def _s5_prompt(z3, dense, d, w_glu, b_glu):
    nb, t_len, _ = z3.shape
    assert 2 * nb == SUBLANE, "state rows [re; im] of all sequences must fill one sublane tile"
    lc = min(128, t_len)
    bdr, bdi, _, _, a1, a2, ccat = dense
    full = lambda shape: pl.BlockSpec(shape, lambda c: (0,) * len(shape))
    kern = functools.partial(_s5_prompt_kernel, nb=nb, lc=lc)
    n_cg = S5_W // LANE
    slabs = lambda a: a.reshape(2 * nb, n_cg, LANE).transpose(1, 0, 2)
    rows = nb * lc
    src = (np.arange(rows) % nb) * lc + np.arange(rows) // nb
    perm_np = np.zeros((rows, rows), np.float32)
    perm_np[np.arange(rows), src] = 1.0
    perm = jnp.asarray(perm_np, BF16)
    unperm = jnp.asarray(np.concatenate([perm_np.T, perm_np.T], axis=1), BF16)
    y, hT = pl.pallas_call(
        kern,
        grid=(t_len // lc,),
        in_specs=[
            pl.BlockSpec((nb, lc, W_BR), lambda c: (0, c, COL_A_U // W_BR)),
            pl.BlockSpec((nb, lc, W_BR), lambda c: (0, c, COL_A_G // W_BR)),
            full((rows, rows)), full((rows, 2 * rows)),
            full((W_BR, S5_W)), full((W_BR, S5_W)), full(ccat.shape),
            full((n_cg, 2 * nb, LANE)), full((n_cg, 2 * nb, LANE)),
            full((1, W_BR)), full((W_BR, W_BR)), full((1, W_BR)),
        ],
        out_specs=[
            pl.BlockSpec((nb, lc, W_BR), lambda c: (0, c, 0)),
            full((n_cg, 2 * nb, LANE)),
        ],
        out_shape=(jax.ShapeDtypeStruct((nb, t_len, W_BR), BF16),
                   jax.ShapeDtypeStruct((n_cg, 2 * nb, LANE), F32)),
        scratch_shapes=[pltpu.VMEM((n_cg, 2 * nb * lc, LANE), F32), pltpu.VMEM((n_cg, 2 * nb, LANE), F32)],
        compiler_params=_params(("arbitrary",)),
        name="s5_prompt",
    )(z3, z3, perm, unperm, bdr, bdi, ccat, slabs(a1), slabs(a2), d.reshape(1, W_BR), w_glu,
      b_glu.reshape(1, W_BR))
    return y, hT.transpose(1, 0, 2).reshape(2 * nb, S5_W)


def _s5_sample_kernel(u_ref, g_ref, h0r_ref, h0i_ref, bdr_ref, bdi_ref, cdr_ref, cdi_ref,
                      ar_ref, ai_ref, d_ref, wglu_ref, bglu_ref, y_ref, hr_ref, hi_ref):
    u = u_ref[...]
    ub = u.astype(BF16)
    ar, ai = ar_ref[...], ai_ref[...]
    h0r, h0i = h0r_ref[...], h0i_ref[...]
    hr = jnp.dot(ub, bdr_ref[...], preferred_element_type=F32) + (ar * h0r - ai * h0i)
    hi = jnp.dot(ub, bdi_ref[...], preferred_element_type=F32) + (ar * h0i + ai * h0r)
    hr_ref[...] = hr
    hi_ref[...] = hi
    y = _bdot(hr, cdr_ref[...]) - _bdot(hi, cdi_ref[...])
    y_ref[...] = _s5_tail(y, u, g_ref[...], d_ref, wglu_ref, bglu_ref).astype(y_ref.dtype)


def _s5_sample(z, h0_re, h0_im, dense, abar_re, abar_im, d, w_glu, b_glu):
    nb = z.shape[0]
    bdr, bdi, cdr, cdi = dense[:4]
    full = lambda shape: pl.BlockSpec(shape, lambda i: (0,) * len(shape))
    st = jax.ShapeDtypeStruct((nb, S5_W), F32)
    return pl.pallas_call(
        _s5_sample_kernel,
        grid=(1,),
        in_specs=[
            pl.BlockSpec((nb, W_BR), lambda i: (0, COL_A_U // W_BR)),
            pl.BlockSpec((nb, W_BR), lambda i: (0, COL_A_G // W_BR)),
            full((nb, S5_W)), full((nb, S5_W)),
            full((W_BR, S5_W)), full((W_BR, S5_W)), full((S5_W, W_BR)), full((S5_W, W_BR)),
            full((1, S5_W)), full((1, S5_W)),
            full((1, W_BR)), full((W_BR, W_BR)), full((1, W_BR)),
        ],
        out_specs=[full((nb, W_BR)), full((nb, S5_W)), full((nb, S5_W))],
        out_shape=(jax.ShapeDtypeStruct((nb, W_BR), BF16), st, st),
        compiler_params=_params(("arbitrary",)),
        name="s5_sample",
    )(z, z, h0_re, h0_im, bdr, bdi, cdr, cdi, abar_re.reshape(1, S5_W), abar_im.reshape(1, S5_W),
      d.reshape(1, W_BR), w_glu, b_glu.reshape(1, W_BR))


def _layernorm(v, g_ref, b_ref):
    mu = jnp.mean(v, axis=-1, keepdims=True)
    vc = v - mu
    var = jnp.mean(vc * vc, axis=-1, keepdims=True)
    return vc * lax.rsqrt(var + LN_EPS) * g_ref[...] + b_ref[...]


def _sgu_prompt_kernel(u_ref, v_ref, g_ref, lng_ref, lnb_ref, ws_ref, bs_ref, y_ref, *, n_chunks):
    cw = W_BR // HG_B
    vn = _layernorm(_gelu(v_ref[...]), lng_ref, lnb_ref)
    row = lax.broadcasted_iota(jnp.int32, (CHUNK, CHUNK), 0)
    col = lax.broadcasted_iota(jnp.int32, (CHUNK, CHUNK), 1)
    ws = [jnp.where(col <= row, ws_ref[h], 0.0).astype(BF16) for h in range(HG_B)]
    bs = bs_ref[...]
    for c in range(n_chunks):
        rows = slice(c * CHUNK, (c + 1) * CHUNK)
        vb = vn[rows, :].astype(BF16)
        mix = jnp.concatenate(
            [jnp.dot(ws[h], vb[:, h * cw:(h + 1) * cw], preferred_element_type=F32) for h in range(HG_B)],
            axis=1) + bs
        out = _gelu(u_ref[rows, :]) * mix * _silu(g_ref[rows, :])
        y_ref[rows, :] = out.astype(y_ref.dtype)


def _sgu_prompt(z, ln_g, ln_b, w_s, b_s):
    n = z.shape[0]
    tm = min(512, n)
    full = lambda shape: pl.BlockSpec(shape, lambda i: (0,) * len(shape))
    bs_full = jnp.repeat(b_s.T, W_BR // HG_B, axis=1)
    kern = functools.partial(_sgu_prompt_kernel, n_chunks=tm // CHUNK)
    col = lambda c: pl.BlockSpec((tm, W_BR), lambda i: (i, c // W_BR))
    return pl.pallas_call(
        kern,
        grid=(n // tm,),
        in_specs=[col(COL_B_U), col(COL_B_V), col(COL_B_G),
                  full((1, W_BR)), full((1, W_BR)), full((HG_B, CHUNK, CHUNK)), full((CHUNK, W_BR))],
        out_specs=pl.BlockSpec((tm, W_BR), lambda i: (i, 0)),
        out_shape=jax.ShapeDtypeStruct((n, W_BR), BF16),
        compiler_params=_params(("arbitrary",)),
        name="sgu_prompt",
    )(z, z, z, ln_g.reshape(1, W_BR), ln_b.reshape(1, W_BR), w_s, bs_full)


def _sgu_sample_kernel(u_ref, v_ref, g_ref, lng_ref, lnb_ref, w00_ref, b0_ref, y_ref, vn_ref):
    vn = _layernorm(_gelu(v_ref[...]), lng_ref, lnb_ref)
    vn_ref[...] = vn
    mix = vn * w00_ref[...] + b0_ref[...]
    y_ref[...] = (_gelu(u_ref[...]) * mix * _silu(g_ref[...])).astype(y_ref.dtype)


def _sgu_sample(z, ln_g, ln_b, w_s, b_s):
    nb = z.shape[0]
    cw = W_BR // HG_B
    full = lambda shape: pl.BlockSpec(shape, lambda i: (0,) * len(shape))
    col = lambda c: pl.BlockSpec((nb, W_BR), lambda i: (0, c // W_BR))
    w00 = jnp.repeat(w_s[:, 0, 0], cw).reshape(1, W_BR)
    b0 = jnp.repeat(b_s[:, 0], cw).reshape(1, W_BR)
    return pl.pallas_call(
        _sgu_sample_kernel,
        grid=(1,),
        in_specs=[col(COL_B_U), col(COL_B_V), col(COL_B_G),
                  full((1, W_BR)), full((1, W_BR)), full((1, W_BR)), full((1, W_BR))],
        out_specs=[full((nb, W_BR)), full((nb, W_BR))],
        out_shape=(jax.ShapeDtypeStruct((nb, W_BR), BF16), jax.ShapeDtypeStruct((nb, W_BR), F32)),
        compiler_params=_params(("arbitrary",)),
        name="sgu_sample",
    )(z, z, z, ln_g.reshape(1, W_BR), ln_b.reshape(1, W_BR), w00, b0)


def _suffix_matrix():
    j = np.arange(LANE)[:, None]
    s = np.arange(LANE)[None, :]
    return jnp.asarray(np.concatenate([(j >= s), np.ones((LANE, LANE), bool)], axis=1), dtype=BF16)


LOG2E = 1.0 / math.log(2.0)


def _softplus2(z):
    return jnp.maximum(z, 0.0) + jnp.log2(1.0 + jnp.exp2(-jnp.abs(z)))


def _sb_prompt_kernel(it_ref, jt_ref, bias_ref, q_ref, k_ref, v_ref, g_ref, m_ref, o_ref,
                      acc0, acc1, car0, car1, *, tq, tk):
    hp = pl.program_id(1)
    p = pl.program_id(2)
    i = it_ref[p]
    j = jt_ref[p]
    nsub = tk // LANE
    accs, cars = (acc0, acc1), (car0, car1)

    @pl.when(j == 0)
    def _():
        for r in (acc0, acc1, car0, car1):
            r[...] = jnp.zeros_like(r)

    def process(masked):
        lane = lax.broadcasted_iota(jnp.int32, (tq, LANE), 1)
        q = q_ref[...] * (HD_C ** -0.5 * LOG2E)
        qh = (jnp.where(lane < HD_C, q, 0.0).astype(BF16), jnp.where(lane >= HD_C, q, 0.0).astype(BF16))
        m = m_ref[...]
        kb = k_ref[...].astype(BF16)
        vb = v_ref[...].astype(BF16)
        n_part = 2 if tq % (2 * SUBLANE) == 0 else 1
        pr = tq // n_part
        chunks = [(h, r) for h in range(2) for r in range(n_part)]
        rows = lambda r: slice(r * pr, (r + 1) * pr)
        if masked:
            valid = [(lax.broadcasted_iota(jnp.int32, (pr, tk), 1)
                      < lax.broadcasted_iota(jnp.int32, (pr, tk), 0) + r * pr) for r in range(n_part)]
        zs = [lax.dot_general(qh[h][rows(r), :], kb, (((1,), (1,)), ((), ())), preferred_element_type=F32)
              + bias_ref[2 * hp + h] * LOG2E for (h, r) in chunks]
        sps = []
        for n, (h, r) in enumerate(chunks):
            sp = _softplus2(zs[n])
            sps.append((jnp.where(valid[r], sp, 0.0) if masked else sp).astype(BF16))
        rss = [[jnp.dot(sps[n][:, sub * LANE:(sub + 1) * LANE], m, preferred_element_type=F32)
                for sub in range(nsub)] for n in range(len(chunks))]
        ebs = []
        for n, (h, r) in enumerate(chunks):
            c = cars[h][rows(r), :]
            es = [None] * nsub
            for sub in reversed(range(nsub)):
                es[sub] = jnp.exp2(zs[n][:, sub * LANE:(sub + 1) * LANE] - rss[n][sub][:, :LANE] - c)
                c = c + rss[n][sub][:, LANE:]
            cars[h][rows(r), :] = c
            e = jnp.concatenate(es, axis=1)
            ebs.append((jnp.where(valid[r], e, 0.0) if masked else e).astype(BF16))
        for n, (h, r) in enumerate(chunks):
            accs[h][rows(r), :] += jnp.dot(ebs[n], vb, preferred_element_type=F32)

    @pl.when(j == 0)
    def _():
        process(True)

    @pl.when(j > 0)
    def _():
        process(False)

    @pl.when(j == i)
    def _():
        lane = lax.broadcasted_iota(jnp.int32, (tq, LANE), 1)
        y = jnp.where(lane < HD_C, acc0[...], acc1[...])
        o_ref[...] = (y * _silu(g_ref[...])).astype(o_ref.dtype)


def _sb_prompt(z, sb_bias, nb, t_len):
    tq = tk = min(512, t_len)
    nq = t_len // tq
    pairs = [(i, j) for i in range(nq) for j in range(i + 1)]
    it = jnp.asarray([p[0] for p in pairs], jnp.int32)
    jt = jnp.asarray([p[1] for p in pairs], jnp.int32)
    kern = functools.partial(_sb_prompt_kernel, tq=tq, tk=tk)
    qspec = lambda c: pl.BlockSpec((tq, LANE), lambda b, h, p, it, jt: (b * nq + it[p], c // LANE + h))
    kspec = lambda c: pl.BlockSpec((tk, LANE), lambda b, h, p, it, jt: (b * nq + it[p] - jt[p], c // LANE + h))
    grid_spec = pltpu.PrefetchScalarGridSpec(
        num_scalar_prefetch=2,
        grid=(nb, H_C // 2, len(pairs)),
        in_specs=[
            pl.BlockSpec(memory_space=pltpu.SMEM),
            qspec(COL_C_Q), kspec(COL_C_K), kspec(COL_C_V), qspec(COL_C_G),
            pl.BlockSpec((LANE, 2 * LANE), lambda b, h, p, it, jt: (0, 0)),
        ],
        out_specs=pl.BlockSpec((tq, LANE), lambda b, h, p, it, jt: (b * nq + it[p], h)),
        scratch_shapes=[pltpu.VMEM((tq, LANE), F32)] * 4,
    )
    return pl.pallas_call(
        kern,
        grid_spec=grid_spec,
        out_shape=jax.ShapeDtypeStruct((nb * t_len, W_BR), BF16),
        compiler_params=_params(("arbitrary",) * 3),
        name="sb_prompt",
    )(it, jt, sb_bias, z, z, z, z, _suffix_matrix())


PAGES_PER_STEP = 16


def _sb_sample_kernel(pt_ref, qb_ref, g_ref, bias_ref, m_ref, *rest):
    k_refs = rest[:PAGES_PER_STEP]
    v_refs = rest[PAGES_PER_STEP:2 * PAGES_PER_STEP]
    o_ref, acc, car = rest[2 * PAGES_PER_STEP:]
    jj = pl.program_id(1)

    @pl.when(jj == 0)
    def _():
        acc[...] = jnp.zeros_like(acc)
        car[...] = jnp.zeros_like(car)

    qb = qb_ref[...] * (HD_C ** -0.5 * LOG2E)
    bias = bias_ref[...] * LOG2E
    m = m_ref[...]
    for r in reversed(range(PAGES_PER_STEP)):
        prod = k_refs[r][...] * qb
        z = jnp.concatenate([jnp.sum(prod[h * HD_C:(h + 1) * HD_C, :], axis=0, keepdims=True)
                             for h in range(H_C)], axis=0) + bias
        rr = _split_dot(_softplus2(z), m, 2)
        e = jnp.exp2(z - rr[:, :LANE] - car[...])
        car[...] += rr[:, LANE:]
        eb = jnp.concatenate([jnp.broadcast_to(e[h:h + 1, :], (HD_C, LANE)) for h in range(H_C)], axis=0)
        acc[...] += v_refs[r][...] * eb

    @pl.when(jj == pl.num_programs(1) - 1)
    def _():
        o_ref[...] = jnp.sum(acc[...], axis=1, keepdims=True) * _silu(g_ref[...])


def _sb_sample(z, cache_k, cache_v, page_table, sb_bias, base):
    nb = z.shape[0]
    n_pages = page_table.shape[1]
    page = cache_k.shape[2]
    assert page == LANE and n_pages % PAGES_PER_STEP == 0
    n_steps = n_pages // PAGES_PER_STEP
    bias = jnp.broadcast_to(sb_bias[:, None], (H_C, page))
    qb = jnp.broadcast_to(z[:, COL_C_Q:COL_C_Q + W_BR, None], (nb, W_BR, page))
    g3 = z[:, COL_C_G:COL_C_G + W_BR].reshape(nb, W_BR, 1)

    def page_spec(r):
        def imap(b, jj, pt):
            return (base + pt[b, (n_steps - 1 - jj) * PAGES_PER_STEP + r], 0, 0)
        return pl.BlockSpec((None, W_BR, page), imap)

    const = lambda shape: pl.BlockSpec(shape, lambda b, jj, pt: (0,) * len(shape))
    col_spec = pl.BlockSpec((None, W_BR, 1), lambda b, jj, pt: (b, 0, 0))
    grid_spec = pltpu.PrefetchScalarGridSpec(
        num_scalar_prefetch=1,
        grid=(nb, n_steps),
        in_specs=[pl.BlockSpec((None, W_BR, page), lambda b, jj, pt: (b, 0, 0)), col_spec,
                  const((H_C, page)), const((LANE, 2 * LANE))]
                 + [page_spec(r) for r in range(PAGES_PER_STEP)]
                 + [page_spec(r) for r in range(PAGES_PER_STEP)],
        out_specs=col_spec,
        scratch_shapes=[pltpu.VMEM((W_BR, page), F32), pltpu.VMEM((H_C, LANE), F32)],
    )
    out = pl.pallas_call(
        _sb_sample_kernel,
        grid_spec=grid_spec,
        out_shape=jax.ShapeDtypeStruct((nb, W_BR, 1), F32),
        compiler_params=_params(("arbitrary", "arbitrary")),
        name="sb_sample",
    )(page_table, qb, g3, bias, _suffix_matrix(),
      *([cache_k] * PAGES_PER_STEP), *([cache_v] * PAGES_PER_STEP))
    return out.reshape(nb, W_BR)


def _head_ones(width):
    idx = np.arange(width) // N_D
    return jnp.asarray(idx[:, None] == idx[None, :], dtype=BF16)


def _rwkv_prep_kernel(r_ref, k_ref, v_ref, wa_ref, pr_ref, pk_ref, pv_ref, pwa_ref,
                      mur_ref, muk_ref, muv_ref, muwa_ref, w0_ref, w2_ref, a0_ref, a2_ref,
                      kk_ref_, ka_ref, rk_ref, eh_ref,
                      ro_ref, ko_ref, vo_ref, dec_ref, kko_ref, bo_ref, bon_ref, *, roll_rows, first_zero):
    def shifted(x_ref, p_ref, mu_ref):
        x = x_ref[...]
        if roll_rows:
            row = lax.broadcasted_iota(jnp.int32, x.shape, 0)
            last = p_ref[SUBLANE - 1:SUBLANE, :]
            if first_zero:
                last = jnp.where(pl.program_id(1) == 0, 0.0, last)
            prev = jnp.where(row == 0, last, pltpu.roll(x, 1, 0))
        else:
            prev = p_ref[...]
        return x + (prev - x) * mu_ref[...]

    r = shifted(r_ref, pr_ref, mur_ref)
    k = shifted(k_ref, pk_ref, muk_ref)
    v = shifted(v_ref, pv_ref, muv_ref)
    wa = shifted(wa_ref, pwa_ref, muwa_ref)
    eh = eh_ref[...]
    w = -_softplus(-(w0_ref[...] + _bdot(jnp.tanh(wa), w2_ref[...]))) - 0.5
    a = jax.nn.sigmoid(a0_ref[...] + _bdot(wa, a2_ref[...]))
    kk = k * kk_ref_[...]
    kk = kk * lax.rsqrt(_split_dot(kk * kk, eh, 3) + 1e-12)
    k2 = k * (1.0 + (a - 1.0) * ka_ref[...])
    ro_ref[...] = r
    ko_ref[...] = k2
    vo_ref[...] = v
    dec_ref[...] = jnp.exp(-jnp.exp(w))
    kko_ref[...] = kk
    bo_ref[...] = kk * a
    bon_ref[...] = _split_dot(r * k2 * rk_ref[...], eh, 3) * v


def _rwkv_prep(z, prev, nb, t_len, mu, w0, w2, a0, a2, k_k, k_a, r_k):
    n = z.shape[0]
    tm = min(512, t_len) if prev is None else n
    bps = max(t_len // tm, 1)
    wa_w = LR_W + LR_A
    row = lambda v: v.reshape(1, -1)
    col = lambda c, w: pl.BlockSpec((tm, w), lambda b, i: (b * bps + i, c // w))
    if prev is None:
        tsub = tm // SUBLANE
        pcol = lambda c, w: pl.BlockSpec(
            (SUBLANE, w), lambda b, i: (jnp.maximum((b * bps + i) * tsub - 1, 0), c // w))
        prev_specs = [pcol(COL_D_R, W_BR), pcol(COL_D_K, W_BR), pcol(COL_D_V, W_BR), pcol(COL_D_WA, wa_w)]
        prev_args = [z, z, z, z]
        grid = (nb, bps)
    else:
        pfull = lambda w: pl.BlockSpec((tm, w), lambda b, i: (0, 0))
        prev_specs = [pfull(W_BR), pfull(W_BR), pfull(W_BR), pfull(wa_w)]
        prev_args = [prev[:, 0:W_BR], prev[:, W_BR:2 * W_BR], prev[:, 2 * W_BR:3 * W_BR], prev[:, 3 * W_BR:]]
        grid = (1, 1)
    full = lambda shape: pl.BlockSpec(shape, lambda b, i: (0,) * len(shape))
    w2p = jnp.concatenate([w2, jnp.zeros((LR_A, W_BR), F32)], axis=0).astype(BF16)
    a2p = jnp.concatenate([jnp.zeros((LR_W, W_BR), F32), a2], axis=0).astype(BF16)
    kern = functools.partial(_rwkv_prep_kernel, roll_rows=prev is None, first_zero=prev is None)
    out = jax.ShapeDtypeStruct((n, W_BR), F32)
    ospec = pl.BlockSpec((tm, W_BR), lambda b, i: (b * bps + i, 0))
    return pl.pallas_call(
        kern,
        grid=grid,
        in_specs=[col(COL_D_R, W_BR), col(COL_D_K, W_BR), col(COL_D_V, W_BR), col(COL_D_WA, wa_w)]
                 + prev_specs
                 + [full((1, W_BR))] * 3 + [full((1, wa_w))]
                 + [full((1, W_BR)), full((wa_w, W_BR)), full((1, W_BR)), full((wa_w, W_BR))]
                 + [full((1, W_BR))] * 3 + [full((W_BR, W_BR))],
        out_specs=[ospec] * 7,
        out_shape=(out,) * 7,
        compiler_params=_params(("arbitrary", "arbitrary")),
        name="rwkv_prep",
    )(z, z, z, z, *prev_args,
      row(mu[0:W_BR]), row(mu[W_BR:2 * W_BR]), row(mu[2 * W_BR:3 * W_BR]), row(mu[3 * W_BR:]),
      row(w0), w2p, row(a0), a2p, row(k_k), row(k_a), row(r_k), _head_ones(W_BR))


def _rwkv_scan_kernel(r_ref, k_ref, v_ref, dec_ref, kk_ref, b_ref, bon_ref, g_ref, s0_ref,
                      lng_ref, lnb_ref, e4_ref, eh_ref, y_ref, sT_ref, s_scr, y_scr,
                      *, nb, lc, state_per_row):
    n_hp = H_D // 2
    c = pl.program_id(0)

    @pl.when(c == 0)
    def _():
        s_scr[...] = s0_ref[...]

    e4 = e4_ref[...]
    sub = lax.broadcasted_iota(jnp.int32, (N_D, LANE), 0)
    lane = lax.broadcasted_iota(jnp.int32, (N_D, LANE), 1)
    diag = (lane % N_D) == sub
    pack = 2 * SUBLANE
    diag_b = diag.astype(F32).astype(BF16).reshape(N_D // pack, pack, LANE)
    combos = [(b, hp) for b in range(nb) for hp in range(n_hp)]
    nc = len(combos)
    assert nc % 2 == 0

    def split2(x):
        hi = x.astype(BF16)
        return jnp.concatenate([hi, (x - hi.astype(F32)).astype(BF16)], axis=1)

    def pairs(slabs):
        return [jnp.concatenate([slabs[q], slabs[q + 1]], axis=1) for q in range(0, len(slabs), 2)]

    def block(out, first_row, n):
        r = first_row + (n // 2) * N_D
        return out[r:r + N_D, (n % 2) * LANE:(n % 2 + 1) * LANE]

    def group(gi, carry):
        r0 = pl.multiple_of(gi * SUBLANE, SUBLANE)
        rows8 = pl.ds(r0, SUBLANE)
        load = lambda ref: [ref[b, rows8, :] for b in range(nb)]
        rr, kx, vx, dx, kkx, bx = (load(x) for x in (r_ref, k_ref, v_ref, dec_ref, kk_ref, b_ref))
        row = lambda blocks, b, hp, tt: blocks[b][tt:tt + 1, hp * LANE:(hp + 1) * LANE]
        state = lambda tt: [s_scr[((r0 + tt) if state_per_row else b) * n_hp + hp] for (b, hp) in combos]
        kk_rows = lambda s, tt: [split2(s[c] * (-row(kkx, b, hp, tt))) for c, (b, hp) in enumerate(combos)]

        def vdiag(tt, b, hp):
            piece = jnp.broadcast_to(row(vx, b, hp, tt), (pack, LANE)).astype(BF16)
            return (diag_b * piece[None]).reshape(N_D, LANE)

        s = state(0)
        out = jnp.dot(jnp.concatenate(
            kk_rows(s, 0) + pairs([vdiag(tt, b, hp) for tt in range(SUBLANE) for (b, hp) in combos]), axis=0),
            e4, preferred_element_type=F32)
        sa = [out[c * N_D:(c + 1) * N_D, :LANE] + out[c * N_D:(c + 1) * N_D, LANE:] for c in range(nc)]
        vb = [block(out, nc * N_D, n) for n in range(SUBLANE * nc)]

        y_rows = [[[None] * n_hp for _ in range(SUBLANE)] for _ in range(nb)]
        for tt in range(SUBLANE):
            last = tt == SUBLANE - 1
            new = []
            for c, (b, hp) in enumerate(combos):
                sn = (s[c] * row(dx, b, hp, tt) + sa[c] * row(bx, b, hp, tt)
                      + vb[tt * nc + c] * row(kx, b, hp, tt))
                if state_per_row or last:
                    s_scr[((r0 + tt) if state_per_row else b) * n_hp + hp] = sn
                new.append(sn)
            y_lhs = pairs([(new[c] * row(rr, b, hp, tt)).astype(BF16) for c, (b, hp) in enumerate(combos)])
            if last:
                lhs, y0 = y_lhs, 0
            else:
                s = state(tt + 1) if state_per_row else new
                lhs, y0 = kk_rows(s, tt + 1) + y_lhs, nc * N_D
            out = jnp.dot(jnp.concatenate(lhs, axis=0), e4, preferred_element_type=F32)
            if not last:
                sa = [out[c * N_D:(c + 1) * N_D, :LANE] + out[c * N_D:(c + 1) * N_D, LANE:] for c in range(nc)]
            for c, (b, hp) in enumerate(combos):
                y_rows[b][tt][hp] = jnp.sum(jnp.where(diag, block(out, y0, c), 0.0), axis=0, keepdims=True)
        for b in range(nb):
            y_scr[b, rows8, :] = jnp.concatenate(
                [jnp.concatenate(y_rows[b][tt], axis=1) for tt in range(SUBLANE)], axis=0)
        return carry

    lax.fori_loop(0, lc // SUBLANE, group, 0)

    sT_ref[...] = s_scr[...]
    eh = eh_ref[...]
    for b in range(nb):
        y = y_scr[b]
        ym = _split_dot(y, eh, 3) * (1.0 / N_D)
        yc = y - ym
        yv = _split_dot(yc * yc, eh, 3) * (1.0 / N_D)
        out = yc * lax.rsqrt(yv + GN_EPS) * lng_ref[...] + lnb_ref[...] + bon_ref[b]
        y_ref[b] = (out * _silu(g_ref[b])).astype(y_ref.dtype)


def _rwkv_scan(prep, z3, s0, lnx_g, lnx_b, state_per_row):
    nb, t_len, _ = z3.shape
    lc = t_len if state_per_row else min(128, t_len)
    p3 = [p.reshape(nb, t_len, W_BR) for p in prep]
    blk = pl.BlockSpec((nb, lc, W_BR), lambda c: (0, c, 0))
    full = lambda shape: pl.BlockSpec(shape, lambda c: (0,) * len(shape))
    kern = functools.partial(_rwkv_scan_kernel, nb=nb, lc=lc, state_per_row=state_per_row)
    n_st = s0.shape[0]
    return pl.pallas_call(
        kern,
        grid=(t_len // lc,),
        in_specs=[blk] * 7
                 + [pl.BlockSpec((nb, lc, W_BR), lambda c: (0, c, COL_D_G // W_BR)),
                    full((n_st, N_D, LANE)), full((1, W_BR)), full((1, W_BR)),
                    full((2 * LANE, 2 * LANE)), full((W_BR, W_BR))],
        out_specs=[blk, full((n_st, N_D, LANE))],
        out_shape=(jax.ShapeDtypeStruct((nb, t_len, W_BR), BF16),
                   jax.ShapeDtypeStruct((n_st, N_D, LANE), F32)),
        scratch_shapes=[pltpu.VMEM((n_st, N_D, LANE), F32), pltpu.VMEM((nb, lc, W_BR), F32)],
        compiler_params=_params(("arbitrary",)),
        name="rwkv_scan",
    )(p3[0], p3[1], p3[2], p3[3], p3[4], p3[5], p3[6], z3, s0,
      lnx_g.reshape(1, W_BR), lnx_b.reshape(1, W_BR), _head_ones(2 * LANE), _head_ones(W_BR))


def _wkv_to_pairs(s):
    nb = s.shape[0]
    return s.reshape(nb, H_D // 2, 2, N_D, N_D).transpose(0, 1, 3, 2, 4).reshape(nb * H_D // 2, N_D, LANE)


def _wkv_from_pairs(s, nb):
    return s.reshape(nb, H_D // 2, N_D, 2, N_D).transpose(0, 1, 3, 2, 4).reshape(nb, H_D, N_D, N_D)


def _merge_kernel(x_ref, ya_ref, yb_ref, yc_ref, yd_ref, mga_ref, mgb_ref, mgc_ref, mgd_ref,
                  gt_ref, wb_ref, wo_ref, fg_ref, *out_refs, final):
    merged = None
    branches = ((ya_ref, mga_ref), (yb_ref, mgb_ref), (yc_ref, mgc_ref), (yd_ref, mgd_ref))
    for n, (y_ref, mg_ref) in enumerate(branches):
        proj = jnp.dot(y_ref[...].astype(BF16), wb_ref[n], preferred_element_type=F32)
        term = jax.nn.sigmoid(mg_ref[...]) * proj
        merged = term if merged is None else merged + term
    x_new = x_ref[...] + gt_ref[...] * _bdot(merged, wo_ref[...])
    out_refs[0][...] = x_new
    if final:
        ms = jnp.mean(x_new * x_new, axis=-1, keepdims=True)
        out_refs[1][...] = x_new * lax.rsqrt(ms + RMS_EPS) * fg_ref[...]


def _merge(x, ys, z, mod, w_branch, w_out, layer, final_g, rows_per_group, final):
    n = x.shape[0]
    tm = min(256, n)
    r = mod.shape[1]
    bpg = rows_per_group // tm
    full = lambda shape: pl.BlockSpec(shape, lambda i: (0,) * len(shape))
    yspec = pl.BlockSpec((tm, W_BR), lambda i: (i, 0))
    xspec = pl.BlockSpec((tm, D_MODEL), lambda i: (i, 0))
    out_sds = jax.ShapeDtypeStruct((n, D_MODEL), F32)
    outs = pl.pallas_call(
        functools.partial(_merge_kernel, final=final),
        grid=(n // tm,),
        in_specs=[xspec, yspec, yspec, yspec, yspec]
                 + [pl.BlockSpec((tm, D_MODEL), lambda i, n=n: (i, COL_M_G // D_MODEL + n)) for n in range(N_BR)]
                 + [pl.BlockSpec((None, r, D_MODEL), lambda i: (i // bpg, 0, 2)),
                  pl.BlockSpec((None, N_BR, W_BR, D_MODEL), lambda i: (layer, 0, 0, 0)),
                  pl.BlockSpec((None, D_MODEL, D_MODEL), lambda i: (layer, 0, 0)), full((1, D_MODEL))],
        out_specs=[xspec, xspec] if final else [xspec],
        out_shape=(out_sds, out_sds) if final else (out_sds,),
        compiler_params=_params(("arbitrary",)),
        name="merge",
    )(x, *ys, z, z, z, z, mod, w_branch, w_out, final_g.reshape(1, D_MODEL))
    return outs


def _pad_w_in(w_in):
    depth = w_in.shape[0]
    pad = jnp.zeros((depth, D_MODEL, COL_D_G - IN_W_SRC_D_G), w_in.dtype)
    return jnp.concatenate([w_in[:, :, :IN_W_SRC_D_G], pad, w_in[:, :, IN_W_SRC_D_G:]], axis=2).astype(BF16)


def kernel(x_prompt, x_sample, cache_k, cache_v, state_ssm_re, state_ssm_im, state_wkv, state_shift, page_table, c_prompt, c_sample, norm_g, w_ada, b_ada, w_in, ssm_a_re, ssm_a_im, ssm_log_dt, ssm_b_re, ssm_b_im, ssm_c_re, ssm_c_im, ssm_d, ssm_w_glu, ssm_b_glu, sgu_ln_g, sgu_ln_b, sgu_w, sgu_b, sb_bias, rwkv_mu, rwkv_w0, rwkv_w2, rwkv_a0, rwkv_a2, rwkv_k_k, rwkv_k_a, rwkv_r_k, rwkv_lnx_g, rwkv_lnx_b, w_branch, w_out, final_norm_g):
    bp, t_len, _ = x_prompt.shape
    db = x_sample.shape[0]
    depth = w_in.shape[0]
    n_pool, page = cache_k.shape[1], cache_k.shape[2]
    np_rows = bp * t_len

    n_c = bp + db
    c_rows = -(-n_c // SUBLANE) * SUBLANE
    c_all = jnp.concatenate([c_prompt, c_sample, jnp.zeros((c_rows - n_c, D_MODEL), F32)], axis=0)
    mod = _ada_mod(c_all, w_ada, b_ada)
    w_in_p = _pad_w_in(w_in)
    w_branch_b = w_branch.astype(BF16)
    w_out_b = w_out.astype(BF16)
    w_glu_b = ssm_w_glu.astype(BF16)
    ck = cache_k.transpose(0, 1, 3, 4, 2).reshape(depth * n_pool, W_BR, page)
    cv = cache_v.transpose(0, 1, 3, 4, 2).reshape(depth * n_pool, W_BR, page)

    xp = x_prompt.reshape(np_rows, D_MODEL)
    xs = x_sample.reshape(db, D_MODEL)
    zero_wkv = jnp.zeros((bp * H_D // 2, N_D, LANE), F32)
    outs_p, outs_s = [], []
    yp = ys_out = None
    for l in range(depth):
        final = l == depth - 1
        mod_p = mod[l, :bp].reshape(bp, 1, 3 * D_MODEL)
        mod_s = mod[l, bp:bp + db].reshape(1, db, 3 * D_MODEL)
        abr, abi, bbr, bbi = _s5_discretise(ssm_a_re[l], ssm_a_im[l], ssm_log_dt[l], ssm_b_re[l], ssm_b_im[l])
        dense = _s5_dense_weights(abr, abi, bbr, bbi, ssm_c_re[l], ssm_c_im[l], bp)
        rw = (rwkv_mu[l], rwkv_w0[l], rwkv_w2[l], rwkv_a0[l], rwkv_a2[l], rwkv_k_k[l], rwkv_k_a[l],
              rwkv_r_k[l].reshape(W_BR))

        z = _in_proj(xp, mod_p, norm_g[l], w_in_p, l, t_len)
        z3 = z.reshape(bp, t_len, IN_W_PAD)
        ya, hT = _s5_prompt(z3, dense, ssm_d[l], w_glu_b[l], ssm_b_glu[l])
        yb = _sgu_prompt(z, sgu_ln_g[l], sgu_ln_b[l], sgu_w[l], sgu_b[l])
        yc = _sb_prompt(z, sb_bias[l], bp, t_len)
        prep = _rwkv_prep(z, None, bp, t_len, *rw)
        yd, sT = _rwkv_scan(prep, z3, zero_wkv, rwkv_lnx_g[l], rwkv_lnx_b[l], False)
        res = _merge(xp, (ya.reshape(np_rows, W_BR), yb, yc, yd.reshape(np_rows, W_BR)), z, mod_p,
                     w_branch_b, w_out_b, l, final_norm_g, t_len, final)
        xp = res[0]
        if final:
            yp = res[1]
        outs_p.append((
            z3[:, :, COL_C_K:COL_C_K + W_BR].reshape(bp, t_len, H_C, HD_C),
            z3[:, :, COL_C_V:COL_C_V + W_BR].reshape(bp, t_len, H_C, HD_C),
            hT[:bp].reshape(bp, G_A, P_A), hT[bp:].reshape(bp, G_A, P_A),
            _wkv_from_pairs(sT, bp),
            z3[:, -1, COL_D_R:COL_D_R + SHIFT_W],
        ))

        zs = _in_proj(xs, mod_s, norm_g[l], w_in_p, l, db)
        sa, hr, hi = _s5_sample(zs, state_ssm_re[l].reshape(db, S5_W), state_ssm_im[l].reshape(db, S5_W),
                                dense, abr, abi, ssm_d[l], w_glu_b[l], ssm_b_glu[l])
        sb, v_rows = _sgu_sample(zs, sgu_ln_g[l], sgu_ln_b[l], sgu_w[l], sgu_b[l])
        sc = _sb_sample(zs, ck, cv, page_table, sb_bias[l], l * n_pool)
        preps = _rwkv_prep(zs, state_shift[l], db, 1, *rw)
        sd, sTs = _rwkv_scan(preps, zs.reshape(1, db, IN_W_PAD), _wkv_to_pairs(state_wkv[l]),
                             rwkv_lnx_g[l], rwkv_lnx_b[l], True)
        res = _merge(xs, (sa, sb, sc, sd.reshape(db, W_BR)), zs, mod_s,
                     w_branch_b, w_out_b, l, final_norm_g, db, final)
        xs = res[0]
        if final:
            ys_out = res[1]
        outs_s.append((
            zs[:, COL_C_K:COL_C_K + W_BR].reshape(db, 1, H_C, HD_C),
            zs[:, COL_C_V:COL_C_V + W_BR].reshape(db, 1, H_C, HD_C),
            hr.reshape(db, G_A, P_A), hi.reshape(db, G_A, P_A),
            _wkv_from_pairs(sTs, db),
            zs[:, COL_D_R:COL_D_R + SHIFT_W],
            v_rows.reshape(db, 1, W_BR),
        ))

    stk = lambda outs, i: jnp.stack([o[i] for o in outs])
    return (yp.reshape(bp, t_len, D_MODEL), ys_out.reshape(db, 1, D_MODEL),
            stk(outs_p, 0), stk(outs_p, 1), stk(outs_s, 0), stk(outs_s, 1),
            stk(outs_p, 2), stk(outs_p, 3), stk(outs_s, 2), stk(outs_s, 3),
            stk(outs_p, 4), stk(outs_s, 4), stk(outs_p, 5), stk(outs_s, 5), stk(outs_s, 6))
```

```python
import functools
import math

import jax
import jax.numpy as jnp
import numpy as np
from jax import lax
from jax.experimental import pallas as pl
from jax.experimental.pallas import tpu as pltpu

F32 = jnp.float32
BF16 = jnp.bfloat16

D_MODEL = 1024
W_BR = D_MODEL // 2
N_BR = 4
CG_A = 16
G_A = W_BR // CG_A
P_A = 64
S5_W = G_A * P_A
S5_BLOCKS = 2
CHUNK = 128
HG_B = 4
H_C = 8
HD_C = W_BR // H_C
H_D = 8
N_D = W_BR // H_D
LR_W = 64
LR_A = 64
SHIFT_W = 3 * W_BR + LR_W + LR_A
GN_EPS = 64e-5
RMS_EPS = 1e-6
LN_EPS = 1e-5
LANE = 128
SUBLANE = 8

COL_A_U, COL_A_G = 0, 512
COL_B_U, COL_B_V, COL_B_G = 1024, 1536, 2048
COL_C_Q, COL_C_K, COL_C_V, COL_C_G = 2560, 3072, 3584, 4096
COL_D_R, COL_D_K, COL_D_V, COL_D_WA = 4608, 5120, 5632, 6144
COL_D_G = 6656
COL_M_G = 7168
IN_W_PAD = COL_M_G + N_BR * D_MODEL
IN_W_SRC_D_G = COL_D_WA + LR_W + LR_A

VMEM_LIMIT = 56 * 2**20


def _params(sem, vmem=VMEM_LIMIT):
    return pltpu.CompilerParams(dimension_semantics=sem, vmem_limit_bytes=vmem)


def _silu(x):
    return x * jax.nn.sigmoid(x)


def _gelu(x):
    c = math.sqrt(2.0 / math.pi)
    return 0.5 * x * (1.0 + jnp.tanh(c * (x + 0.044715 * (x * x * x))))


def _softplus(x):
    return jnp.maximum(x, 0.0) + jnp.log(1.0 + jnp.exp(-jnp.abs(x)))


def _bdot(a, b):
    return jnp.dot(a.astype(BF16), b.astype(BF16), preferred_element_type=F32)


def _split_dot(x, w, passes):
    pieces = []
    rem = x
    for p in range(passes):
        piece = rem.astype(BF16)
        pieces.append(piece)
        if p + 1 < passes:
            rem = rem - piece.astype(F32)
    if passes == 1:
        return jnp.dot(pieces[0], w, preferred_element_type=F32)
    return jnp.dot(jnp.concatenate(pieces, axis=1), jnp.concatenate([w] * passes, axis=0),
                   preferred_element_type=F32)


def _ada_kernel(c_ref, w_ref, b_ref, o_ref):
    o_ref[...] = _bdot(_silu(c_ref[...]), w_ref[...]) + b_ref[...]


def _ada_mod(c_all, w_ada, b_ada):
    depth = w_ada.shape[0]
    rows = c_all.shape[0]
    return pl.pallas_call(
        _ada_kernel,
        grid=(depth, 3),
        in_specs=[
            pl.BlockSpec((rows, D_MODEL), lambda l, k: (0, 0)),
            pl.BlockSpec((None, D_MODEL, D_MODEL), lambda l, k: (l, 0, k)),
            pl.BlockSpec((None, 1, D_MODEL), lambda l, k: (l, 0, k)),
        ],
        out_specs=pl.BlockSpec((None, rows, D_MODEL), lambda l, k: (l, 0, k)),
        out_shape=jax.ShapeDtypeStruct((depth, rows, 3 * D_MODEL), F32),
        compiler_params=_params(("arbitrary", "arbitrary")),
        name="ada_mod",
    )(c_all, w_ada, b_ada.reshape(depth, 1, 3 * D_MODEL))


def _inproj_kernel(x_ref, g_ref, sh_ref, sc_ref, w_ref, z_ref, h_scr):
    @pl.when(pl.program_id(1) == 0)
    def _():
        x = x_ref[...]
        ms = jnp.mean(x * x, axis=-1, keepdims=True)
        y = x * lax.rsqrt(ms + RMS_EPS) * g_ref[...]
        h_scr[...] = (y * (1.0 + sc_ref[...]) + sh_ref[...]).astype(BF16)

    z_ref[...] = jnp.dot(h_scr[...], w_ref[...], preferred_element_type=F32)


def _in_proj(x, mod, norm_g, w_in, layer, rows_per_group):
    n = x.shape[0]
    tm = min(2048, rows_per_group)
    tn = 1024
    r = mod.shape[1]
    bpg = rows_per_group // tm
    mod_spec = lambda k: pl.BlockSpec((None, r, D_MODEL), lambda i, j: (i // bpg, 0, k))
    return pl.pallas_call(
        _inproj_kernel,
        grid=(n // tm, IN_W_PAD // tn),
        in_specs=[
            pl.BlockSpec((tm, D_MODEL), lambda i, j: (i, 0)),
            pl.BlockSpec((1, D_MODEL), lambda i, j: (0, 0)),
            mod_spec(0),
            mod_spec(1),
            pl.BlockSpec((None, D_MODEL, tn), lambda i, j: (layer, 0, j)),
        ],
        out_specs=pl.BlockSpec((tm, tn), lambda i, j: (i, j)),
        out_shape=jax.ShapeDtypeStruct((n, IN_W_PAD), F32),
        scratch_shapes=[pltpu.VMEM((tm, D_MODEL), BF16)],
        compiler_params=_params(("arbitrary", "arbitrary")),
        name="in_proj",
    )(x, norm_g.reshape(1, D_MODEL), mod, mod, w_in)


def _s5_disc_kernel(ar_ref, ai_ref, ldt_ref, br_ref, bi_ref,
                    abr_ref, abi_ref, bbr_ref, bbi_ref):
    ar, ai = ar_ref[...], ai_ref[...]
    dt = jnp.exp(ldt_ref[...])
    mag = jnp.exp(ar * dt)
    abar_re, abar_im = mag * jnp.cos(ai * dt), mag * jnp.sin(ai * dt)
    den = ar * ar + ai * ai
    xr, xi = abar_re - 1.0, abar_im
    coef_re = (xr * ar + xi * ai) / den
    coef_im = (xi * ar - xr * ai) / den
    br, bi = br_ref[...], bi_ref[...]
    abr_ref[...] = abar_re
    abi_ref[...] = abar_im
    bbr_ref[...] = coef_re * br - coef_im * bi
    bbi_ref[...] = coef_re * bi + coef_im * br


def _s5_discretise(a_re, a_im, log_dt, b_re, b_im):
    gp = jax.ShapeDtypeStruct((G_A, 1, P_A), F32)
    gcp = jax.ShapeDtypeStruct((G_A, CG_A, P_A), F32)
    return pl.pallas_call(
        _s5_disc_kernel,
        out_shape=(gp, gp, gcp, gcp),
        name="s5_discretise",
    )(a_re.reshape(G_A, 1, P_A), a_im.reshape(G_A, 1, P_A), log_dt.reshape(G_A, 1, 1),
      jnp.swapaxes(b_re, 1, 2), jnp.swapaxes(b_im, 1, 2))


def _s5_dense_weights(abar_re, abar_im, bbar_re, bbar_im, c_re, c_im, nb):
    eye = jnp.eye(G_A, dtype=F32)
    bd = lambda m: jnp.einsum('gcp,gh->gchp', m, eye).reshape(W_BR, S5_W).astype(BF16)
    cd = lambda m: jnp.einsum('gcp,gh->gphc', m, eye).reshape(S5_W, W_BR).astype(BF16)
    ar = abar_re.reshape(1, S5_W)
    ai = abar_im.reshape(1, S5_W)
    a1 = jnp.broadcast_to(ar, (2 * nb, S5_W))
    a2 = jnp.concatenate([jnp.broadcast_to(-ai, (nb, S5_W)), jnp.broadcast_to(ai, (nb, S5_W))], axis=0)
    cdr, cdi = cd(c_re), cd(c_im)
    cw, sw = W_BR // S5_BLOCKS, S5_W // S5_BLOCKS
    ccat = jnp.stack([jnp.concatenate([cdr[f * sw:(f + 1) * sw, f * cw:(f + 1) * cw],
                                       -cdi[f * sw:(f + 1) * sw, f * cw:(f + 1) * cw]], axis=0)
                      for f in range(S5_BLOCKS)])
    return bd(bbar_re), bd(bbar_im), cdr, cdi, a1, a2, ccat


def _s5_tail(y, u, g, d_ref, wglu_ref, bglu_ref):
    y = y + d_ref[...] * u
    y = y * jax.nn.sigmoid(_bdot(_gelu(y), wglu_ref[...]) + bglu_ref[...])
    return y * _silu(g)


def _s5_prompt_kernel(u_ref, g_ref, perm_ref, unperm_ref, bdr_ref, bdi_ref, ccat_ref, a1_ref, a2_ref,
                      d_ref, wglu_ref, bglu_ref, y_ref, hT_ref, hs_scr, h_scr, *, nb, lc):
    c = pl.program_id(0)

    @pl.when(c == 0)
    def _():
        h_scr[...] = jnp.zeros_like(h_scr)

    rows = nb * lc
    n_cg = S5_W // LANE
    u = u_ref[...].reshape(rows, W_BR)
    ub = jnp.dot(perm_ref[...], u.astype(BF16), preferred_element_type=F32).astype(BF16)
    cw, sw = W_BR // S5_BLOCKS, S5_W // S5_BLOCKS
    for half, bd_ref in enumerate((bdr_ref, bdi_ref)):
        for f in range(S5_BLOCKS):
            bu = jnp.dot(ub[:, f * cw:(f + 1) * cw], bd_ref[f * cw:(f + 1) * cw, f * sw:(f + 1) * sw],
                         preferred_element_type=F32)
            for c in range(sw // LANE):
                cg = f * (sw // LANE) + c
                hs_scr[cg, half * rows:(half + 1) * rows, :] = bu[:, c * LANE:(c + 1) * LANE]
    a1 = [a1_ref[cg] for cg in range(n_cg)]
    a2 = [a2_ref[cg] for cg in range(n_cg)]

    top = lax.broadcasted_iota(jnp.int32, (2 * nb, LANE), 0) < nb

    def two_steps(tp, hs):
        r_re = pl.ds(pl.multiple_of(tp * 2 * nb, SUBLANE), 2 * nb)
        r_im = pl.ds(pl.multiple_of(rows + tp * 2 * nb, SUBLANE), 2 * nb)
        out = []
        for cg in range(n_cg):
            xr, xi = hs_scr[cg, r_re, :], hs_scr[cg, r_im, :]
            bu0 = jnp.where(top, xr, pltpu.roll(xi, nb, 0))
            bu1 = jnp.where(top, pltpu.roll(xr, nb, 0), xi)
            h1 = a1[cg] * hs[cg] + a2[cg] * pltpu.roll(hs[cg], nb, 0) + bu0
            h1s = pltpu.roll(h1, nb, 0)
            h2 = a1[cg] * h1 + a2[cg] * h1s + bu1
            hs_scr[cg, r_re, :] = jnp.where(top, h1, pltpu.roll(h2, nb, 0))
            hs_scr[cg, r_im, :] = jnp.where(top, h1s, h2)
            out.append(h2)
        return tuple(out)

    hs = lax.fori_loop(0, lc // 2, two_steps, tuple(h_scr[cg] for cg in range(n_cg)))
    for cg in range(n_cg):
        h_scr[cg] = hs[cg]
        hT_ref[cg] = hs[cg]
    ys = []
    for f in range(S5_BLOCKS):
        cgs = range(f * (sw // LANE), (f + 1) * (sw // LANE))
        h_re = jnp.concatenate([hs_scr[cg, 0:rows, :] for cg in cgs], axis=1).astype(BF16)
        h_im = jnp.concatenate([hs_scr[cg, rows:2 * rows, :] for cg in cgs], axis=1).astype(BF16)
        ys.append(jnp.dot(jnp.concatenate([h_re, h_im], axis=1), ccat_ref[f], preferred_element_type=F32))
    y_tb = jnp.concatenate(ys, axis=1)
    y_hi = y_tb.astype(BF16)
    y_lo = (y_tb - y_hi.astype(F32)).astype(BF16)
    y = jnp.dot(unperm_ref[...], jnp.concatenate([y_hi, y_lo], axis=0), preferred_element_type=F32)
    g = g_ref[...].reshape(rows, W_BR)
    out = _s5_tail(y, u, g, d_ref, wglu_ref, bglu_ref)
    y_ref[...] = out.reshape(nb, lc, W_BR).astype(y_ref.dtype)


def _s5_prompt(z3, dense, d, w_glu, b_glu):
    nb, t_len, _ = z3.shape
    assert 2 * nb == SUBLANE, "state rows [re; im] of all sequences must fill one sublane tile"
    lc = min(128, t_len)
    bdr, bdi, _, _, a1, a2, ccat = dense
    full = lambda shape: pl.BlockSpec(shape, lambda c: (0,) * len(shape))
    kern = functools.partial(_s5_prompt_kernel, nb=nb, lc=lc)
    n_cg = S5_W // LANE
    slabs = lambda a: a.reshape(2 * nb, n_cg, LANE).transpose(1, 0, 2)
    rows = nb * lc
    src = (np.arange(rows) % nb) * lc + np.arange(rows) // nb
    perm_np = np.zeros((rows, rows), np.float32)
    perm_np[np.arange(rows), src] = 1.0
    perm = jnp.asarray(perm_np, BF16)
    unperm = jnp.asarray(np.concatenate([perm_np.T, perm_np.T], axis=1), BF16)
    y, hT = pl.pallas_call(
        kern,
        grid=(t_len // lc,),
        in_specs=[
            pl.BlockSpec((nb, lc, W_BR), lambda c: (0, c, COL_A_U // W_BR)),
            pl.BlockSpec((nb, lc, W_BR), lambda c: (0, c, COL_A_G // W_BR)),
            full((rows, rows)), full((rows, 2 * rows)),
            full((W_BR, S5_W)), full((W_BR, S5_W)), full(ccat.shape),
            full((n_cg, 2 * nb, LANE)), full((n_cg, 2 * nb, LANE)),
            full((1, W_BR)), full((W_BR, W_BR)), full((1, W_BR)),
        ],
        out_specs=[
            pl.BlockSpec((nb, lc, W_BR), lambda c: (0, c, 0)),
            full((n_cg, 2 * nb, LANE)),
        ],
        out_shape=(jax.ShapeDtypeStruct((nb, t_len, W_BR), BF16),
                   jax.ShapeDtypeStruct((n_cg, 2 * nb, LANE), F32)),
        scratch_shapes=[pltpu.VMEM((n_cg, 2 * nb * lc, LANE), F32), pltpu.VMEM((n_cg, 2 * nb, LANE), F32)],
        compiler_params=_params(("arbitrary",)),
        name="s5_prompt",
    )(z3, z3, perm, unperm, bdr, bdi, ccat, slabs(a1), slabs(a2), d.reshape(1, W_BR), w_glu,
      b_glu.reshape(1, W_BR))
    return y, hT.transpose(1, 0, 2).reshape(2 * nb, S5_W)


def _s5_sample_kernel(u_ref, g_ref, h0r_ref, h0i_ref, bdr_ref, bdi_ref, cdr_ref, cdi_ref,
                      ar_ref, ai_ref, d_ref, wglu_ref, bglu_ref, y_ref, hr_ref, hi_ref):
    u = u_ref[...]
    ub = u.astype(BF16)
    ar, ai = ar_ref[...], ai_ref[...]
    h0r, h0i = h0r_ref[...], h0i_ref[...]
    hr = jnp.dot(ub, bdr_ref[...], preferred_element_type=F32) + (ar * h0r - ai * h0i)
    hi = jnp.dot(ub, bdi_ref[...], preferred_element_type=F32) + (ar * h0i + ai * h0r)
    hr_ref[...] = hr
    hi_ref[...] = hi
    y = _bdot(hr, cdr_ref[...]) - _bdot(hi, cdi_ref[...])
    y_ref[...] = _s5_tail(y, u, g_ref[...], d_ref, wglu_ref, bglu_ref).astype(y_ref.dtype)


def _s5_sample(z, h0_re, h0_im, dense, abar_re, abar_im, d, w_glu, b_glu):
    nb = z.shape[0]
    bdr, bdi, cdr, cdi = dense[:4]
    full = lambda shape: pl.BlockSpec(shape, lambda i: (0,) * len(shape))
    st = jax.ShapeDtypeStruct((nb, S5_W), F32)
    return pl.pallas_call(
        _s5_sample_kernel,
        grid=(1,),
        in_specs=[
            pl.BlockSpec((nb, W_BR), lambda i: (0, COL_A_U // W_BR)),
            pl.BlockSpec((nb, W_BR), lambda i: (0, COL_A_G // W_BR)),
            full((nb, S5_W)), full((nb, S5_W)),
            full((W_BR, S5_W)), full((W_BR, S5_W)), full((S5_W, W_BR)), full((S5_W, W_BR)),
            full((1, S5_W)), full((1, S5_W)),
            full((1, W_BR)), full((W_BR, W_BR)), full((1, W_BR)),
        ],
        out_specs=[full((nb, W_BR)), full((nb, S5_W)), full((nb, S5_W))],
        out_shape=(jax.ShapeDtypeStruct((nb, W_BR), BF16), st, st),
        compiler_params=_params(("arbitrary",)),
        name="s5_sample",
    )(z, z, h0_re, h0_im, bdr, bdi, cdr, cdi, abar_re.reshape(1, S5_W), abar_im.reshape(1, S5_W),
      d.reshape(1, W_BR), w_glu, b_glu.reshape(1, W_BR))


def _layernorm(v, g_ref, b_ref):
    mu = jnp.mean(v, axis=-1, keepdims=True)
    vc = v - mu
    var = jnp.mean(vc * vc, axis=-1, keepdims=True)
    return vc * lax.rsqrt(var + LN_EPS) * g_ref[...] + b_ref[...]


def _sgu_prompt_kernel(u_ref, v_ref, g_ref, lng_ref, lnb_ref, ws_ref, bs_ref, y_ref, *, n_chunks):
    cw = W_BR // HG_B
    vn = _layernorm(_gelu(v_ref[...]), lng_ref, lnb_ref)
    row = lax.broadcasted_iota(jnp.int32, (CHUNK, CHUNK), 0)
    col = lax.broadcasted_iota(jnp.int32, (CHUNK, CHUNK), 1)
    ws = [jnp.where(col <= row, ws_ref[h], 0.0).astype(BF16) for h in range(HG_B)]
    bs = bs_ref[...]
    for c in range(n_chunks):
        rows = slice(c * CHUNK, (c + 1) * CHUNK)
        vb = vn[rows, :].astype(BF16)
        mix = jnp.concatenate(
            [jnp.dot(ws[h], vb[:, h * cw:(h + 1) * cw], preferred_element_type=F32) for h in range(HG_B)],
            axis=1) + bs
        out = _gelu(u_ref[rows, :]) * mix * _silu(g_ref[rows, :])
        y_ref[rows, :] = out.astype(y_ref.dtype)


def _sgu_prompt(z, ln_g, ln_b, w_s, b_s):
    n = z.shape[0]
    tm = min(512, n)
    full = lambda shape: pl.BlockSpec(shape, lambda i: (0,) * len(shape))
    bs_full = jnp.repeat(b_s.T, W_BR // HG_B, axis=1)
    kern = functools.partial(_sgu_prompt_kernel, n_chunks=tm // CHUNK)
    col = lambda c: pl.BlockSpec((tm, W_BR), lambda i: (i, c // W_BR))
    return pl.pallas_call(
        kern,
        grid=(n // tm,),
        in_specs=[col(COL_B_U), col(COL_B_V), col(COL_B_G),
                  full((1, W_BR)), full((1, W_BR)), full((HG_B, CHUNK, CHUNK)), full((CHUNK, W_BR))],
        out_specs=pl.BlockSpec((tm, W_BR), lambda i: (i, 0)),
        out_shape=jax.ShapeDtypeStruct((n, W_BR), BF16),
        compiler_params=_params(("arbitrary",)),
        name="sgu_prompt",
    )(z, z, z, ln_g.reshape(1, W_BR), ln_b.reshape(1, W_BR), w_s, bs_full)


def _sgu_sample_kernel(u_ref, v_ref, g_ref, lng_ref, lnb_ref, w00_ref, b0_ref, y_ref, vn_ref):
    vn = _layernorm(_gelu(v_ref[...]), lng_ref, lnb_ref)
    vn_ref[...] = vn
    mix = vn * w00_ref[...] + b0_ref[...]
    y_ref[...] = (_gelu(u_ref[...]) * mix * _silu(g_ref[...])).astype(y_ref.dtype)


def _sgu_sample(z, ln_g, ln_b, w_s, b_s):
    nb = z.shape[0]
    cw = W_BR // HG_B
    full = lambda shape: pl.BlockSpec(shape, lambda i: (0,) * len(shape))
    col = lambda c: pl.BlockSpec((nb, W_BR), lambda i: (0, c // W_BR))
    w00 = jnp.repeat(w_s[:, 0, 0], cw).reshape(1, W_BR)
    b0 = jnp.repeat(b_s[:, 0], cw).reshape(1, W_BR)
    return pl.pallas_call(
        _sgu_sample_kernel,
        grid=(1,),
        in_specs=[col(COL_B_U), col(COL_B_V), col(COL_B_G),
                  full((1, W_BR)), full((1, W_BR)), full((1, W_BR)), full((1, W_BR))],
        out_specs=[full((nb, W_BR)), full((nb, W_BR))],
        out_shape=(jax.ShapeDtypeStruct((nb, W_BR), BF16), jax.ShapeDtypeStruct((nb, W_BR), F32)),
        compiler_params=_params(("arbitrary",)),
        name="sgu_sample",
    )(z, z, z, ln_g.reshape(1, W_BR), ln_b.reshape(1, W_BR), w00, b0)


def _suffix_matrix():
    j = np.arange(LANE)[:, None]
    s = np.arange(LANE)[None, :]
    return jnp.asarray(np.concatenate([(j >= s), np.ones((LANE, LANE), bool)], axis=1), dtype=BF16)


LOG2E = 1.0 / math.log(2.0)


def _softplus2(z):
    return jnp.maximum(z, 0.0) + jnp.log2(1.0 + jnp.exp2(-jnp.abs(z)))


def _sb_prompt_kernel(it_ref, jt_ref, bias_ref, q_ref, k_ref, v_ref, g_ref, m_ref, o_ref,
                      acc0, acc1, car0, car1, *, tq, tk):
    hp = pl.program_id(1)
    p = pl.program_id(2)
    i = it_ref[p]
    j = jt_ref[p]
    nsub = tk // LANE
    accs, cars = (acc0, acc1), (car0, car1)

    @pl.when(j == 0)
    def _():
        for r in (acc0, acc1, car0, car1):
            r[...] = jnp.zeros_like(r)

    def process(masked):
        lane = lax.broadcasted_iota(jnp.int32, (tq, LANE), 1)
        q = q_ref[...] * (HD_C ** -0.5 * LOG2E)
        qh = (jnp.where(lane < HD_C, q, 0.0).astype(BF16), jnp.where(lane >= HD_C, q, 0.0).astype(BF16))
        m = m_ref[...]
        kb = k_ref[...].astype(BF16)
        vb = v_ref[...].astype(BF16)
        n_part = 2 if tq % (2 * SUBLANE) == 0 else 1
        pr = tq // n_part
        chunks = [(h, r) for h in range(2) for r in range(n_part)]
        rows = lambda r: slice(r * pr, (r + 1) * pr)
        if masked:
            valid = [(lax.broadcasted_iota(jnp.int32, (pr, tk), 1)
                      < lax.broadcasted_iota(jnp.int32, (pr, tk), 0) + r * pr) for r in range(n_part)]
        zs = [lax.dot_general(qh[h][rows(r), :], kb, (((1,), (1,)), ((), ())), preferred_element_type=F32)
              + bias_ref[2 * hp + h] * LOG2E for (h, r) in chunks]
        sps = []
        for n, (h, r) in enumerate(chunks):
            sp = _softplus2(zs[n])
            sps.append((jnp.where(valid[r], sp, 0.0) if masked else sp).astype(BF16))
        rss = [[jnp.dot(sps[n][:, sub * LANE:(sub + 1) * LANE], m, preferred_element_type=F32)
                for sub in range(nsub)] for n in range(len(chunks))]
        ebs = []
        for n, (h, r) in enumerate(chunks):
            c = cars[h][rows(r), :]
            es = [None] * nsub
            for sub in reversed(range(nsub)):
                es[sub] = jnp.exp2(zs[n][:, sub * LANE:(sub + 1) * LANE] - rss[n][sub][:, :LANE] - c)
                c = c + rss[n][sub][:, LANE:]
            cars[h][rows(r), :] = c
            e = jnp.concatenate(es, axis=1)
            ebs.append((jnp.where(valid[r], e, 0.0) if masked else e).astype(BF16))
        for n, (h, r) in enumerate(chunks):
            accs[h][rows(r), :] += jnp.dot(ebs[n], vb, preferred_element_type=F32)

    @pl.when(j == 0)
    def _():
        process(True)

    @pl.when(j > 0)
    def _():
        process(False)

    @pl.when(j == i)
    def _():
        lane = lax.broadcasted_iota(jnp.int32, (tq, LANE), 1)
        y = jnp.where(lane < HD_C, acc0[...], acc1[...])
        o_ref[...] = (y * _silu(g_ref[...])).astype(o_ref.dtype)


def _sb_prompt(z, sb_bias, nb, t_len):
    tq = tk = min(512, t_len)
    nq = t_len // tq
    pairs = [(i, j) for i in range(nq) for j in range(i + 1)]
    it = jnp.asarray([p[0] for p in pairs], jnp.int32)
    jt = jnp.asarray([p[1] for p in pairs], jnp.int32)
    kern = functools.partial(_sb_prompt_kernel, tq=tq, tk=tk)
    qspec = lambda c: pl.BlockSpec((tq, LANE), lambda b, h, p, it, jt: (b * nq + it[p], c // LANE + h))
    kspec = lambda c: pl.BlockSpec((tk, LANE), lambda b, h, p, it, jt: (b * nq + it[p] - jt[p], c // LANE + h))
    grid_spec = pltpu.PrefetchScalarGridSpec(
        num_scalar_prefetch=2,
        grid=(nb, H_C // 2, len(pairs)),
        in_specs=[
            pl.BlockSpec(memory_space=pltpu.SMEM),
            qspec(COL_C_Q), kspec(COL_C_K), kspec(COL_C_V), qspec(COL_C_G),
            pl.BlockSpec((LANE, 2 * LANE), lambda b, h, p, it, jt: (0, 0)),
        ],
        out_specs=pl.BlockSpec((tq, LANE), lambda b, h, p, it, jt: (b * nq + it[p], h)),
        scratch_shapes=[pltpu.VMEM((tq, LANE), F32)] * 4,
    )
    return pl.pallas_call(
        kern,
        grid_spec=grid_spec,
        out_shape=jax.ShapeDtypeStruct((nb * t_len, W_BR), BF16),
        compiler_params=_params(("arbitrary",) * 3),
        name="sb_prompt",
    )(it, jt, sb_bias, z, z, z, z, _suffix_matrix())


PAGES_PER_STEP = 16


def _sb_sample_kernel(pt_ref, qb_ref, g_ref, bias_ref, m_ref, *rest):
    k_refs = rest[:PAGES_PER_STEP]
    v_refs = rest[PAGES_PER_STEP:2 * PAGES_PER_STEP]
    o_ref, acc, car = rest[2 * PAGES_PER_STEP:]
    jj = pl.program_id(1)

    @pl.when(jj == 0)
    def _():
        acc[...] = jnp.zeros_like(acc)
        car[...] = jnp.zeros_like(car)

    qb = qb_ref[...] * (HD_C ** -0.5 * LOG2E)
    bias = bias_ref[...] * LOG2E
    m = m_ref[...]
    for r in reversed(range(PAGES_PER_STEP)):
        prod = k_refs[r][...] * qb
        z = jnp.concatenate([jnp.sum(prod[h * HD_C:(h + 1) * HD_C, :], axis=0, keepdims=True)
                             for h in range(H_C)], axis=0) + bias
        rr = _split_dot(_softplus2(z), m, 2)
        e = jnp.exp2(z - rr[:, :LANE] - car[...])
        car[...] += rr[:, LANE:]
        eb = jnp.concatenate([jnp.broadcast_to(e[h:h + 1, :], (HD_C, LANE)) for h in range(H_C)], axis=0)
        acc[...] += v_refs[r][...] * eb

    @pl.when(jj == pl.num_programs(1) - 1)
    def _():
        o_ref[...] = jnp.sum(acc[...], axis=1, keepdims=True) * _silu(g_ref[...])


def _sb_sample(z, cache_k, cache_v, page_table, sb_bias, base):
    nb = z.shape[0]
    n_pages = page_table.shape[1]
    page = cache_k.shape[2]
    assert page == LANE and n_pages % PAGES_PER_STEP == 0
    n_steps = n_pages // PAGES_PER_STEP
    bias = jnp.broadcast_to(sb_bias[:, None], (H_C, page))
    qb = jnp.broadcast_to(z[:, COL_C_Q:COL_C_Q + W_BR, None], (nb, W_BR, page))
    g3 = z[:, COL_C_G:COL_C_G + W_BR].reshape(nb, W_BR, 1)

    def page_spec(r):
        def imap(b, jj, pt):
            return (base + pt[b, (n_steps - 1 - jj) * PAGES_PER_STEP + r], 0, 0)
        return pl.BlockSpec((None, W_BR, page), imap)

    const = lambda shape: pl.BlockSpec(shape, lambda b, jj, pt: (0,) * len(shape))
    col_spec = pl.BlockSpec((None, W_BR, 1), lambda b, jj, pt: (b, 0, 0))
    grid_spec = pltpu.PrefetchScalarGridSpec(
        num_scalar_prefetch=1,
        grid=(nb, n_steps),
        in_specs=[pl.BlockSpec((None, W_BR, page), lambda b, jj, pt: (b, 0, 0)), col_spec,
                  const((H_C, page)), const((LANE, 2 * LANE))]
                 + [page_spec(r) for r in range(PAGES_PER_STEP)]
                 + [page_spec(r) for r in range(PAGES_PER_STEP)],
        out_specs=col_spec,
        scratch_shapes=[pltpu.VMEM((W_BR, page), F32), pltpu.VMEM((H_C, LANE), F32)],
    )
    out = pl.pallas_call(
        _sb_sample_kernel,
        grid_spec=grid_spec,
        out_shape=jax.ShapeDtypeStruct((nb, W_BR, 1), F32),
        compiler_params=_params(("arbitrary", "arbitrary")),
        name="sb_sample",
    )(page_table, qb, g3, bias, _suffix_matrix(),
      *([cache_k] * PAGES_PER_STEP), *([cache_v] * PAGES_PER_STEP))
    return out.reshape(nb, W_BR)


def _head_ones(width):
    idx = np.arange(width) // N_D
    return jnp.asarray(idx[:, None] == idx[None, :], dtype=BF16)


def _rwkv_prep_kernel(r_ref, k_ref, v_ref, wa_ref, pr_ref, pk_ref, pv_ref, pwa_ref,
                      mur_ref, muk_ref, muv_ref, muwa_ref, w0_ref, w2_ref, a0_ref, a2_ref,
                      kk_ref_, ka_ref, rk_ref, eh_ref,
                      ro_ref, ko_ref, vo_ref, dec_ref, kko_ref, bo_ref, bon_ref, *, roll_rows, first_zero):
    def shifted(x_ref, p_ref, mu_ref):
        x = x_ref[...]
        if roll_rows:
            row = lax.broadcasted_iota(jnp.int32, x.shape, 0)
            last = p_ref[SUBLANE - 1:SUBLANE, :]
            if first_zero:
                last = jnp.where(pl.program_id(1) == 0, 0.0, last)
            prev = jnp.where(row == 0, last, pltpu.roll(x, 1, 0))
        else:
            prev = p_ref[...]
        return x + (prev - x) * mu_ref[...]

    r = shifted(r_ref, pr_ref, mur_ref)
    k = shifted(k_ref, pk_ref, muk_ref)
    v = shifted(v_ref, pv_ref, muv_ref)
    wa = shifted(wa_ref, pwa_ref, muwa_ref)
    eh = eh_ref[...]
    w = -_softplus(-(w0_ref[...] + _bdot(jnp.tanh(wa), w2_ref[...]))) - 0.5
    a = jax.nn.sigmoid(a0_ref[...] + _bdot(wa, a2_ref[...]))
    kk = k * kk_ref_[...]
    kk = kk * lax.rsqrt(_split_dot(kk * kk, eh, 3) + 1e-12)
    k2 = k * (1.0 + (a - 1.0) * ka_ref[...])
    ro_ref[...] = r
    ko_ref[...] = k2
    vo_ref[...] = v
    dec_ref[...] = jnp.exp(-jnp.exp(w))
    kko_ref[...] = kk
    bo_ref[...] = kk * a
    bon_ref[...] = _split_dot(r * k2 * rk_ref[...], eh, 3) * v


def _rwkv_prep(z, prev, nb, t_len, mu, w0, w2, a0, a2, k_k, k_a, r_k):
    n = z.shape[0]
    tm = min(512, t_len) if prev is None else n
    bps = max(t_len // tm, 1)
    wa_w = LR_W + LR_A
    row = lambda v: v.reshape(1, -1)
    col = lambda c, w: pl.BlockSpec((tm, w), lambda b, i: (b * bps + i, c // w))
    if prev is None:
        tsub = tm // SUBLANE
        pcol = lambda c, w: pl.BlockSpec(
            (SUBLANE, w), lambda b, i: (jnp.maximum((b * bps + i) * tsub - 1, 0), c // w))
        prev_specs = [pcol(COL_D_R, W_BR), pcol(COL_D_K, W_BR), pcol(COL_D_V, W_BR), pcol(COL_D_WA, wa_w)]
        prev_args = [z, z, z, z]
        grid = (nb, bps)
    else:
        pfull = lambda w: pl.BlockSpec((tm, w), lambda b, i: (0, 0))
        prev_specs = [pfull(W_BR), pfull(W_BR), pfull(W_BR), pfull(wa_w)]
        prev_args = [prev[:, 0:W_BR], prev[:, W_BR:2 * W_BR], prev[:, 2 * W_BR:3 * W_BR], prev[:, 3 * W_BR:]]
        grid = (1, 1)
    full = lambda shape: pl.BlockSpec(shape, lambda b, i: (0,) * len(shape))
    w2p = jnp.concatenate([w2, jnp.zeros((LR_A, W_BR), F32)], axis=0).astype(BF16)
    a2p = jnp.concatenate([jnp.zeros((LR_W, W_BR), F32), a2], axis=0).astype(BF16)
    kern = functools.partial(_rwkv_prep_kernel, roll_rows=prev is None, first_zero=prev is None)
    out = jax.ShapeDtypeStruct((n, W_BR), F32)
    ospec = pl.BlockSpec((tm, W_BR), lambda b, i: (b * bps + i, 0))
    return pl.pallas_call(
        kern,
        grid=grid,
        in_specs=[col(COL_D_R, W_BR), col(COL_D_K, W_BR), col(COL_D_V, W_BR), col(COL_D_WA, wa_w)]
                 + prev_specs
                 + [full((1, W_BR))] * 3 + [full((1, wa_w))]
                 + [full((1, W_BR)), full((wa_w, W_BR)), full((1, W_BR)), full((wa_w, W_BR))]
                 + [full((1, W_BR))] * 3 + [full((W_BR, W_BR))],
        out_specs=[ospec] * 7,
        out_shape=(out,) * 7,
        compiler_params=_params(("arbitrary", "arbitrary")),
        name="rwkv_prep",
    )(z, z, z, z, *prev_args,
      row(mu[0:W_BR]), row(mu[W_BR:2 * W_BR]), row(mu[2 * W_BR:3 * W_BR]), row(mu[3 * W_BR:]),
      row(w0), w2p, row(a0), a2p, row(k_k), row(k_a), row(r_k), _head_ones(W_BR))


def _rwkv_scan_kernel(r_ref, k_ref, v_ref, dec_ref, kk_ref, b_ref, bon_ref, g_ref, s0_ref,
                      lng_ref, lnb_ref, e4_ref, eh_ref, y_ref, sT_ref, s_scr, y_scr,
                      *, nb, lc, state_per_row):
    n_hp = H_D // 2
    c = pl.program_id(0)

    @pl.when(c == 0)
    def _():
        s_scr[...] = s0_ref[...]

    e4 = e4_ref[...]
    sub = lax.broadcasted_iota(jnp.int32, (N_D, LANE), 0)
    lane = lax.broadcasted_iota(jnp.int32, (N_D, LANE), 1)
    diag = (lane % N_D) == sub
    pack = 2 * SUBLANE
    diag_b = diag.astype(F32).astype(BF16).reshape(N_D // pack, pack, LANE)
    combos = [(b, hp) for b in range(nb) for hp in range(n_hp)]
    nc = len(combos)
    assert nc % 2 == 0

    def pairs(slabs):
        return [jnp.concatenate([slabs[q], slabs[q + 1]], axis=1) for q in range(0, len(slabs), 2)]

    def block(out, first_row, n):
        r = first_row + (n // 2) * N_D
        return out[r:r + N_D, (n % 2) * LANE:(n % 2 + 1) * LANE]

    def group(gi, carry):
        r0 = pl.multiple_of(gi * SUBLANE, SUBLANE)
        rows8 = pl.ds(r0, SUBLANE)
        load = lambda ref: [ref[b, rows8, :] for b in range(nb)]
        rr, kx, vx, dx, kkx, bx = (load(x) for x in (r_ref, k_ref, v_ref, dec_ref, kk_ref, b_ref))
        row = lambda blocks, b, hp, tt: blocks[b][tt:tt + 1, hp * LANE:(hp + 1) * LANE]
        state = lambda tt: [s_scr[((r0 + tt) if state_per_row else b) * n_hp + hp] for (b, hp) in combos]
        kk_rows = lambda s, tt: pairs([(s[c] * (-row(kkx, b, hp, tt))).astype(BF16)
                                       for c, (b, hp) in enumerate(combos)])

        def vdiag(tt, b, hp):
            piece = jnp.broadcast_to(row(vx, b, hp, tt), (pack, LANE)).astype(BF16)
            return (diag_b * piece[None]).reshape(N_D, LANE)

        s = state(0)
        out = jnp.dot(jnp.concatenate(
            kk_rows(s, 0) + pairs([vdiag(tt, b, hp) for tt in range(SUBLANE) for (b, hp) in combos]), axis=0),
            e4, preferred_element_type=F32)
        sa = [block(out, 0, c) for c in range(nc)]
        vb = [block(out, nc // 2 * N_D, n) for n in range(SUBLANE * nc)]

        y_rows = [[[None] * n_hp for _ in range(SUBLANE)] for _ in range(nb)]
        for tt in range(SUBLANE):
            last = tt == SUBLANE - 1
            new = []
            for c, (b, hp) in enumerate(combos):
                sn = (s[c] * row(dx, b, hp, tt) + sa[c] * row(bx, b, hp, tt)
                      + vb[tt * nc + c] * row(kx, b, hp, tt))
                if state_per_row or last:
                    s_scr[((r0 + tt) if state_per_row else b) * n_hp + hp] = sn
                new.append(sn)
            y_lhs = pairs([(new[c] * row(rr, b, hp, tt)).astype(BF16) for c, (b, hp) in enumerate(combos)])
            if last:
                lhs, y0 = y_lhs, 0
            else:
                s = state(tt + 1) if state_per_row else new
                lhs, y0 = kk_rows(s, tt + 1) + y_lhs, nc // 2 * N_D
            out = jnp.dot(jnp.concatenate(lhs, axis=0), e4, preferred_element_type=F32)
            if not last:
                sa = [block(out, 0, c) for c in range(nc)]
            for c, (b, hp) in enumerate(combos):
                y_rows[b][tt][hp] = jnp.sum(jnp.where(diag, block(out, y0, c), 0.0), axis=0, keepdims=True)
        for b in range(nb):
            y_scr[b, rows8, :] = jnp.concatenate(
                [jnp.concatenate(y_rows[b][tt], axis=1) for tt in range(SUBLANE)], axis=0)
        return carry

    lax.fori_loop(0, lc // SUBLANE, group, 0)

    sT_ref[...] = s_scr[...]
    eh = eh_ref[...]
    for b in range(nb):
        y = y_scr[b]
        ym = _split_dot(y, eh, 2) * (1.0 / N_D)
        yc = y - ym
        yv = _split_dot(yc * yc, eh, 2) * (1.0 / N_D)
        out = yc * lax.rsqrt(yv + GN_EPS) * lng_ref[...] + lnb_ref[...] + bon_ref[b]
        y_ref[b] = (out * _silu(g_ref[b])).astype(y_ref.dtype)


def _rwkv_scan(prep, z3, s0, lnx_g, lnx_b, state_per_row):
    nb, t_len, _ = z3.shape
    lc = t_len if state_per_row else min(128, t_len)
    p3 = [p.reshape(nb, t_len, W_BR) for p in prep]
    blk = pl.BlockSpec((nb, lc, W_BR), lambda c: (0, c, 0))
    full = lambda shape: pl.BlockSpec(shape, lambda c: (0,) * len(shape))
    kern = functools.partial(_rwkv_scan_kernel, nb=nb, lc=lc, state_per_row=state_per_row)
    n_st = s0.shape[0]
    return pl.pallas_call(
        kern,
        grid=(t_len // lc,),
        in_specs=[blk] * 7
                 + [pl.BlockSpec((nb, lc, W_BR), lambda c: (0, c, COL_D_G // W_BR)),
                    full((n_st, N_D, LANE)), full((1, W_BR)), full((1, W_BR)),
                    full((2 * LANE, 2 * LANE)), full((W_BR, W_BR))],
        out_specs=[blk, full((n_st, N_D, LANE))],
        out_shape=(jax.ShapeDtypeStruct((nb, t_len, W_BR), BF16),
                   jax.ShapeDtypeStruct((n_st, N_D, LANE), F32)),
        scratch_shapes=[pltpu.VMEM((n_st, N_D, LANE), F32), pltpu.VMEM((nb, lc, W_BR), F32)],
        compiler_params=_params(("arbitrary",)),
        name="rwkv_scan",
    )(p3[0], p3[1], p3[2], p3[3], p3[4], p3[5], p3[6], z3, s0,
      lnx_g.reshape(1, W_BR), lnx_b.reshape(1, W_BR), _head_ones(2 * LANE), _head_ones(W_BR))


def _wkv_to_pairs(s):
    nb = s.shape[0]
    return s.reshape(nb, H_D // 2, 2, N_D, N_D).transpose(0, 1, 3, 2, 4).reshape(nb * H_D // 2, N_D, LANE)


def _wkv_from_pairs(s, nb):
    return s.reshape(nb, H_D // 2, N_D, 2, N_D).transpose(0, 1, 3, 2, 4).reshape(nb, H_D, N_D, N_D)


def _merge_kernel(x_ref, ya_ref, yb_ref, yc_ref, yd_ref, mga_ref, mgb_ref, mgc_ref, mgd_ref,
                  gt_ref, wb_ref, wo_ref, fg_ref, *out_refs, final):
    merged = None
    branches = ((ya_ref, mga_ref), (yb_ref, mgb_ref), (yc_ref, mgc_ref), (yd_ref, mgd_ref))
    for n, (y_ref, mg_ref) in enumerate(branches):
        proj = jnp.dot(y_ref[...].astype(BF16), wb_ref[n], preferred_element_type=F32)
        term = jax.nn.sigmoid(mg_ref[...]) * proj
        merged = term if merged is None else merged + term
    x_new = x_ref[...] + gt_ref[...] * _bdot(merged, wo_ref[...])
    out_refs[0][...] = x_new
    if final:
        ms = jnp.mean(x_new * x_new, axis=-1, keepdims=True)
        out_refs[1][...] = x_new * lax.rsqrt(ms + RMS_EPS) * fg_ref[...]


def _merge(x, ys, z, mod, w_branch, w_out, layer, final_g, rows_per_group, final):
    n = x.shape[0]
    tm = min(256, n)
    r = mod.shape[1]
    bpg = rows_per_group // tm
    full = lambda shape: pl.BlockSpec(shape, lambda i: (0,) * len(shape))
    yspec = pl.BlockSpec((tm, W_BR), lambda i: (i, 0))
    xspec = pl.BlockSpec((tm, D_MODEL), lambda i: (i, 0))
    out_sds = jax.ShapeDtypeStruct((n, D_MODEL), F32)
    outs = pl.pallas_call(
        functools.partial(_merge_kernel, final=final),
        grid=(n // tm,),
        in_specs=[xspec, yspec, yspec, yspec, yspec]
                 + [pl.BlockSpec((tm, D_MODEL), lambda i, n=n: (i, COL_M_G // D_MODEL + n)) for n in range(N_BR)]
                 + [pl.BlockSpec((None, r, D_MODEL), lambda i: (i // bpg, 0, 2)),
                  pl.BlockSpec((None, N_BR, W_BR, D_MODEL), lambda i: (layer, 0, 0, 0)),
                  pl.BlockSpec((None, D_MODEL, D_MODEL), lambda i: (layer, 0, 0)), full((1, D_MODEL))],
        out_specs=[xspec, xspec] if final else [xspec],
        out_shape=(out_sds, out_sds) if final else (out_sds,),
        compiler_params=_params(("arbitrary",)),
        name="merge",
    )(x, *ys, z, z, z, z, mod, w_branch, w_out, final_g.reshape(1, D_MODEL))
    return outs


def _pad_w_in(w_in):
    depth = w_in.shape[0]
    pad = jnp.zeros((depth, D_MODEL, COL_D_G - IN_W_SRC_D_G), w_in.dtype)
    return jnp.concatenate([w_in[:, :, :IN_W_SRC_D_G], pad, w_in[:, :, IN_W_SRC_D_G:]], axis=2).astype(BF16)


def kernel(x_prompt, x_sample, cache_k, cache_v, state_ssm_re, state_ssm_im, state_wkv, state_shift, page_table, c_prompt, c_sample, norm_g, w_ada, b_ada, w_in, ssm_a_re, ssm_a_im, ssm_log_dt, ssm_b_re, ssm_b_im, ssm_c_re, ssm_c_im, ssm_d, ssm_w_glu, ssm_b_glu, sgu_ln_g, sgu_ln_b, sgu_w, sgu_b, sb_bias, rwkv_mu, rwkv_w0, rwkv_w2, rwkv_a0, rwkv_a2, rwkv_k_k, rwkv_k_a, rwkv_r_k, rwkv_lnx_g, rwkv_lnx_b, w_branch, w_out, final_norm_g):
    bp, t_len, _ = x_prompt.shape
    db = x_sample.shape[0]
    depth = w_in.shape[0]
    n_pool, page = cache_k.shape[1], cache_k.shape[2]
    np_rows = bp * t_len

    n_c = bp + db
    c_rows = -(-n_c // SUBLANE) * SUBLANE
    c_all = jnp.concatenate([c_prompt, c_sample, jnp.zeros((c_rows - n_c, D_MODEL), F32)], axis=0)
    mod = _ada_mod(c_all, w_ada, b_ada)
    w_in_p = _pad_w_in(w_in)
    w_branch_b = w_branch.astype(BF16)
    w_out_b = w_out.astype(BF16)
    w_glu_b = ssm_w_glu.astype(BF16)
    ck = cache_k.transpose(0, 1, 3, 4, 2).reshape(depth * n_pool, W_BR, page)
    cv = cache_v.transpose(0, 1, 3, 4, 2).reshape(depth * n_pool, W_BR, page)

    xp = x_prompt.reshape(np_rows, D_MODEL)
    xs = x_sample.reshape(db, D_MODEL)
    zero_wkv = jnp.zeros((bp * H_D // 2, N_D, LANE), F32)
    outs_p, outs_s = [], []
    yp = ys_out = None
    for l in range(depth):
        final = l == depth - 1
        mod_p = mod[l, :bp].reshape(bp, 1, 3 * D_MODEL)
        mod_s = mod[l, bp:bp + db].reshape(1, db, 3 * D_MODEL)
        abr, abi, bbr, bbi = _s5_discretise(ssm_a_re[l], ssm_a_im[l], ssm_log_dt[l], ssm_b_re[l], ssm_b_im[l])
        dense = _s5_dense_weights(abr, abi, bbr, bbi, ssm_c_re[l], ssm_c_im[l], bp)
        rw = (rwkv_mu[l], rwkv_w0[l], rwkv_w2[l], rwkv_a0[l], rwkv_a2[l], rwkv_k_k[l], rwkv_k_a[l],
              rwkv_r_k[l].reshape(W_BR))

        z = _in_proj(xp, mod_p, norm_g[l], w_in_p, l, t_len)
        z3 = z.reshape(bp, t_len, IN_W_PAD)
        ya, hT = _s5_prompt(z3, dense, ssm_d[l], w_glu_b[l], ssm_b_glu[l])
        yb = _sgu_prompt(z, sgu_ln_g[l], sgu_ln_b[l], sgu_w[l], sgu_b[l])
        yc = _sb_prompt(z, sb_bias[l], bp, t_len)
        prep = _rwkv_prep(z, None, bp, t_len, *rw)
        yd, sT = _rwkv_scan(prep, z3, zero_wkv, rwkv_lnx_g[l], rwkv_lnx_b[l], False)
        res = _merge(xp, (ya.reshape(np_rows, W_BR), yb, yc, yd.reshape(np_rows, W_BR)), z, mod_p,
                     w_branch_b, w_out_b, l, final_norm_g, t_len, final)
        xp = res[0]
        if final:
            yp = res[1]
        outs_p.append((
            z3[:, :, COL_C_K:COL_C_K + W_BR].reshape(bp, t_len, H_C, HD_C),
            z3[:, :, COL_C_V:COL_C_V + W_BR].reshape(bp, t_len, H_C, HD_C),
            hT[:bp].reshape(bp, G_A, P_A), hT[bp:].reshape(bp, G_A, P_A),
            _wkv_from_pairs(sT, bp),
            z3[:, -1, COL_D_R:COL_D_R + SHIFT_W],
        ))

        zs = _in_proj(xs, mod_s, norm_g[l], w_in_p, l, db)
        sa, hr, hi = _s5_sample(zs, state_ssm_re[l].reshape(db, S5_W), state_ssm_im[l].reshape(db, S5_W),
                                dense, abr, abi, ssm_d[l], w_glu_b[l], ssm_b_glu[l])
        sb, v_rows = _sgu_sample(zs, sgu_ln_g[l], sgu_ln_b[l], sgu_w[l], sgu_b[l])
        sc = _sb_sample(zs, ck, cv, page_table, sb_bias[l], l * n_pool)
        preps = _rwkv_prep(zs, state_shift[l], db, 1, *rw)
        sd, sTs = _rwkv_scan(preps, zs.reshape(1, db, IN_W_PAD), _wkv_to_pairs(state_wkv[l]),
                             rwkv_lnx_g[l], rwkv_lnx_b[l], True)
        res = _merge(xs, (sa, sb, sc, sd.reshape(db, W_BR)), zs, mod_s,
                     w_branch_b, w_out_b, l, final_norm_g, db, final)
        xs = res[0]
        if final:
            ys_out = res[1]
        outs_s.append((
            zs[:, COL_C_K:COL_C_K + W_BR].reshape(db, 1, H_C, HD_C),
            zs[:, COL_C_V:COL_C_V + W_BR].reshape(db, 1, H_C, HD_C),
            hr.reshape(db, G_A, P_A), hi.reshape(db, G_A, P_A),
            _wkv_from_pairs(sTs, db),
            zs[:, COL_D_R:COL_D_R + SHIFT_W],
            v_rows.reshape(db, 1, W_BR),
        ))

    stk = lambda outs, i: jnp.stack([o[i] for o in outs])
    return (yp.reshape(bp, t_len, D_MODEL), ys_out.reshape(db, 1, D_MODEL),
            stk(outs_p, 0), stk(outs_p, 1), stk(outs_s, 0), stk(outs_s, 1),
            stk(outs_p, 2), stk(outs_p, 3), stk(outs_s, 2), stk(outs_s, 3),
            stk(outs_p, 4), stk(outs_s, 4), stk(outs_p, 5), stk(outs_s, 5), stk(outs_s, 6))
```

```python
import functools
import math

import jax
import jax.numpy as jnp
import numpy as np
from jax import lax
from jax.experimental import pallas as pl
from jax.experimental.pallas import tpu as pltpu

F32 = jnp.float32
BF16 = jnp.bfloat16

D_MODEL = 1024
W_BR = D_MODEL // 2
N_BR = 4
CG_A = 16
G_A = W_BR // CG_A
P_A = 64
S5_W = G_A * P_A
S5_BLOCKS = 2
CHUNK = 128
HG_B = 4
H_C = 8
HD_C = W_BR // H_C
H_D = 8
N_D = W_BR // H_D
LR_W = 64
LR_A = 64
SHIFT_W = 3 * W_BR + LR_W + LR_A
GN_EPS = 64e-5
RMS_EPS = 1e-6
LN_EPS = 1e-5
LANE = 128
SUBLANE = 8

COL_A_U, COL_A_G = 0, 512
COL_B_U, COL_B_V, COL_B_G = 1024, 1536, 2048
COL_C_Q, COL_C_K, COL_C_V, COL_C_G = 2560, 3072, 3584, 4096
COL_D_R, COL_D_K, COL_D_V, COL_D_WA = 4608, 5120, 5632, 6144
COL_D_G = 6656
COL_M_G = 7168
IN_W_PAD = COL_M_G + N_BR * D_MODEL
IN_W_SRC_D_G = COL_D_WA + LR_W + LR_A

VMEM_LIMIT = 56 * 2**20


def _params(sem, vmem=VMEM_LIMIT):
    return pltpu.CompilerParams(dimension_semantics=sem, vmem_limit_bytes=vmem)


def _silu(x):
    return x * jax.nn.sigmoid(x)


def _gelu(x):
    c = math.sqrt(2.0 / math.pi)
    return 0.5 * x * (1.0 + jnp.tanh(c * (x + 0.044715 * (x * x * x))))


def _softplus(x):
    return jnp.maximum(x, 0.0) + jnp.log(1.0 + jnp.exp(-jnp.abs(x)))


def _bdot(a, b):
    return jnp.dot(a.astype(BF16), b.astype(BF16), preferred_element_type=F32)


def _split_dot(x, w, passes):
    pieces = []
    rem = x
    for p in range(passes):
        piece = rem.astype(BF16)
        pieces.append(piece)
        if p + 1 < passes:
            rem = rem - piece.astype(F32)
    if passes == 1:
        return jnp.dot(pieces[0], w, preferred_element_type=F32)
    return jnp.dot(jnp.concatenate(pieces, axis=1), jnp.concatenate([w] * passes, axis=0),
                   preferred_element_type=F32)


def _ada_kernel(c_ref, w_ref, b_ref, o_ref):
    o_ref[...] = _bdot(_silu(c_ref[...]), w_ref[...]) + b_ref[...]


def _ada_mod(c_all, w_ada, b_ada):
    depth = w_ada.shape[0]
    rows = c_all.shape[0]
    return pl.pallas_call(
        _ada_kernel,
        grid=(depth, 3),
        in_specs=[
            pl.BlockSpec((rows, D_MODEL), lambda l, k: (0, 0)),
            pl.BlockSpec((None, D_MODEL, D_MODEL), lambda l, k: (l, 0, k)),
            pl.BlockSpec((None, 1, D_MODEL), lambda l, k: (l, 0, k)),
        ],
        out_specs=pl.BlockSpec((None, rows, D_MODEL), lambda l, k: (l, 0, k)),
        out_shape=jax.ShapeDtypeStruct((depth, rows, 3 * D_MODEL), F32),
        compiler_params=_params(("arbitrary", "arbitrary")),
        name="ada_mod",
    )(c_all, w_ada, b_ada.reshape(depth, 1, 3 * D_MODEL))


def _inproj_kernel(x_ref, g_ref, sh_ref, sc_ref, w_ref, z_ref, h_scr):
    @pl.when(pl.program_id(1) == 0)
    def _():
        x = x_ref[...]
        ms = jnp.mean(x * x, axis=-1, keepdims=True)
        y = x * lax.rsqrt(ms + RMS_EPS) * g_ref[...]
        h_scr[...] = (y * (1.0 + sc_ref[...]) + sh_ref[...]).astype(BF16)

    z_ref[...] = jnp.dot(h_scr[...], w_ref[...], preferred_element_type=F32)


def _in_proj(x, mod, norm_g, w_in, layer, rows_per_group):
    n = x.shape[0]
    tm = min(2048, rows_per_group)
    tn = 1024
    r = mod.shape[1]
    bpg = rows_per_group // tm
    mod_spec = lambda k: pl.BlockSpec((None, r, D_MODEL), lambda i, j: (i // bpg, 0, k))
    return pl.pallas_call(
        _inproj_kernel,
        grid=(n // tm, IN_W_PAD // tn),
        in_specs=[
            pl.BlockSpec((tm, D_MODEL), lambda i, j: (i, 0)),
            pl.BlockSpec((1, D_MODEL), lambda i, j: (0, 0)),
            mod_spec(0),
            mod_spec(1),
            pl.BlockSpec((None, D_MODEL, tn), lambda i, j: (layer, 0, j)),
        ],
        out_specs=pl.BlockSpec((tm, tn), lambda i, j: (i, j)),
        out_shape=jax.ShapeDtypeStruct((n, IN_W_PAD), F32),
        scratch_shapes=[pltpu.VMEM((tm, D_MODEL), BF16)],
        compiler_params=_params(("arbitrary", "arbitrary")),
        name="in_proj",
    )(x, norm_g.reshape(1, D_MODEL), mod, mod, w_in)


def _s5_disc_kernel(ar_ref, ai_ref, ldt_ref, br_ref, bi_ref,
                    abr_ref, abi_ref, bbr_ref, bbi_ref):
    ar, ai = ar_ref[...], ai_ref[...]
    dt = jnp.exp(ldt_ref[...])
    mag = jnp.exp(ar * dt)
    abar_re, abar_im = mag * jnp.cos(ai * dt), mag * jnp.sin(ai * dt)
    den = ar * ar + ai * ai
    xr, xi = abar_re - 1.0, abar_im
    coef_re = (xr * ar + xi * ai) / den
    coef_im = (xi * ar - xr * ai) / den
    br, bi = br_ref[...], bi_ref[...]
    abr_ref[...] = abar_re
    abi_ref[...] = abar_im
    bbr_ref[...] = coef_re * br - coef_im * bi
    bbi_ref[...] = coef_re * bi + coef_im * br


def _s5_discretise(a_re, a_im, log_dt, b_re, b_im):
    gp = jax.ShapeDtypeStruct((G_A, 1, P_A), F32)
    gcp = jax.ShapeDtypeStruct((G_A, CG_A, P_A), F32)
    return pl.pallas_call(
        _s5_disc_kernel,
        out_shape=(gp, gp, gcp, gcp),
        name="s5_discretise",
    )(a_re.reshape(G_A, 1, P_A), a_im.reshape(G_A, 1, P_A), log_dt.reshape(G_A, 1, 1),
      jnp.swapaxes(b_re, 1, 2), jnp.swapaxes(b_im, 1, 2))


def _s5_dense_weights(abar_re, abar_im, bbar_re, bbar_im, c_re, c_im, nb):
    eye = jnp.eye(G_A, dtype=F32)
    bd = lambda m: jnp.einsum('gcp,gh->gchp', m, eye).reshape(W_BR, S5_W).astype(BF16)
    cd = lambda m: jnp.einsum('gcp,gh->gphc', m, eye).reshape(S5_W, W_BR).astype(BF16)
    ar = abar_re.reshape(1, S5_W)
    ai = abar_im.reshape(1, S5_W)
    a1 = jnp.broadcast_to(ar, (2 * nb, S5_W))
    a2 = jnp.concatenate([jnp.broadcast_to(-ai, (nb, S5_W)), jnp.broadcast_to(ai, (nb, S5_W))], axis=0)
    cdr, cdi = cd(c_re), cd(c_im)
    cw, sw = W_BR // S5_BLOCKS, S5_W // S5_BLOCKS
    ccat = jnp.stack([jnp.concatenate([cdr[f * sw:(f + 1) * sw, f * cw:(f + 1) * cw],
                                       -cdi[f * sw:(f + 1) * sw, f * cw:(f + 1) * cw]], axis=0)
                      for f in range(S5_BLOCKS)])
    return bd(bbar_re), bd(bbar_im), cdr, cdi, a1, a2, ccat


def _s5_tail(y, u, g, d_ref, wglu_ref, bglu_ref):
    y = y + d_ref[...] * u
    y = y * jax.nn.sigmoid(_bdot(_gelu(y), wglu_ref[...]) + bglu_ref[...])
    return y * _silu(g)


def _s5_prompt_kernel(u_ref, g_ref, perm_ref, unperm_ref, bdr_ref, bdi_ref, ccat_ref, a1_ref, a2_ref,
                      d_ref, wglu_ref, bglu_ref, y_ref, hT_ref, hs_scr, h_scr, *, nb, lc):
    c = pl.program_id(0)

    @pl.when(c == 0)
    def _():
        h_scr[...] = jnp.zeros_like(h_scr)

    rows = nb * lc
    n_cg = S5_W // LANE
    u = u_ref[...].reshape(rows, W_BR)
    ub = jnp.dot(perm_ref[...], u.astype(BF16), preferred_element_type=F32).astype(BF16)
    cw, sw = W_BR // S5_BLOCKS, S5_W // S5_BLOCKS
    for half, bd_ref in enumerate((bdr_ref, bdi_ref)):
        for f in range(S5_BLOCKS):
            bu = jnp.dot(ub[:, f * cw:(f + 1) * cw], bd_ref[f * cw:(f + 1) * cw, f * sw:(f + 1) * sw],
                         preferred_element_type=F32)
            for c in range(sw // LANE):
                cg = f * (sw // LANE) + c
                hs_scr[cg, half * rows:(half + 1) * rows, :] = bu[:, c * LANE:(c + 1) * LANE]
    a1 = [a1_ref[cg] for cg in range(n_cg)]
    a2 = [a2_ref[cg] for cg in range(n_cg)]

    top = lax.broadcasted_iota(jnp.int32, (2 * nb, LANE), 0) < nb

    def two_steps(tp, hs):
        r_re = pl.ds(pl.multiple_of(tp * 2 * nb, SUBLANE), 2 * nb)
        r_im = pl.ds(pl.multiple_of(rows + tp * 2 * nb, SUBLANE), 2 * nb)
        out = []
        for cg in range(n_cg):
            xr, xi = hs_scr[cg, r_re, :], hs_scr[cg, r_im, :]
            bu0 = jnp.where(top, xr, pltpu.roll(xi, nb, 0))
            bu1 = jnp.where(top, pltpu.roll(xr, nb, 0), xi)
            h1 = a1[cg] * hs[cg] + a2[cg] * pltpu.roll(hs[cg], nb, 0) + bu0
            h1s = pltpu.roll(h1, nb, 0)
            h2 = a1[cg] * h1 + a2[cg] * h1s + bu1
            hs_scr[cg, r_re, :] = jnp.where(top, h1, pltpu.roll(h2, nb, 0))
            hs_scr[cg, r_im, :] = jnp.where(top, h1s, h2)
            out.append(h2)
        return tuple(out)

    hs = lax.fori_loop(0, lc // 2, two_steps, tuple(h_scr[cg] for cg in range(n_cg)))
    for cg in range(n_cg):
        h_scr[cg] = hs[cg]
        hT_ref[cg] = hs[cg]
    ys = []
    for f in range(S5_BLOCKS):
        cgs = range(f * (sw // LANE), (f + 1) * (sw // LANE))
        h_re = jnp.concatenate([hs_scr[cg, 0:rows, :] for cg in cgs], axis=1).astype(BF16)
        h_im = jnp.concatenate([hs_scr[cg, rows:2 * rows, :] for cg in cgs], axis=1).astype(BF16)
        ys.append(jnp.dot(jnp.concatenate([h_re, h_im], axis=1), ccat_ref[f], preferred_element_type=F32))
    y_tb = jnp.concatenate(ys, axis=1)
    y_hi = y_tb.astype(BF16)
    y_lo = (y_tb - y_hi.astype(F32)).astype(BF16)
    y = jnp.dot(unperm_ref[...], jnp.concatenate([y_hi, y_lo], axis=0), preferred_element_type=F32)
    g = g_ref[...].reshape(rows, W_BR)
    out = _s5_tail(y, u, g, d_ref, wglu_ref, bglu_ref)
    y_ref[...] = out.reshape(nb, lc, W_BR).astype(y_ref.dtype)


def _s5_prompt(z3, dense, d, w_glu, b_glu):
    nb, t_len, _ = z3.shape
    assert 2 * nb == SUBLANE, "state rows [re; im] of all sequences must fill one sublane tile"
    lc = min(128, t_len)
    bdr, bdi, _, _, a1, a2, ccat = dense
    full = lambda shape: pl.BlockSpec(shape, lambda c: (0,) * len(shape))
    kern = functools.partial(_s5_prompt_kernel, nb=nb, lc=lc)
    n_cg = S5_W // LANE
    slabs = lambda a: a.reshape(2 * nb, n_cg, LANE).transpose(1, 0, 2)
    rows = nb * lc
    src = (np.arange(rows) % nb) * lc + np.arange(rows) // nb
    perm_np = np.zeros((rows, rows), np.float32)
    perm_np[np.arange(rows), src] = 1.0
    perm = jnp.asarray(perm_np, BF16)
    unperm = jnp.asarray(np.concatenate([perm_np.T, perm_np.T], axis=1), BF16)
    y, hT = pl.pallas_call(
        kern,
        grid=(t_len // lc,),
        in_specs=[
            pl.BlockSpec((nb, lc, W_BR), lambda c: (0, c, COL_A_U // W_BR)),
            pl.BlockSpec((nb, lc, W_BR), lambda c: (0, c, COL_A_G // W_BR)),
            full((rows, rows)), full((rows, 2 * rows)),
            full((W_BR, S5_W)), full((W_BR, S5_W)), full(ccat.shape),
            full((n_cg, 2 * nb, LANE)), full((n_cg, 2 * nb, LANE)),
            full((1, W_BR)), full((W_BR, W_BR)), full((1, W_BR)),
        ],
        out_specs=[
            pl.BlockSpec((nb, lc, W_BR), lambda c: (0, c, 0)),
            full((n_cg, 2 * nb, LANE)),
        ],
        out_shape=(jax.ShapeDtypeStruct((nb, t_len, W_BR), BF16),
                   jax.ShapeDtypeStruct((n_cg, 2 * nb, LANE), F32)),
        scratch_shapes=[pltpu.VMEM((n_cg, 2 * nb * lc, LANE), F32), pltpu.VMEM((n_cg, 2 * nb, LANE), F32)],
        compiler_params=_params(("arbitrary",)),
        name="s5_prompt",
    )(z3, z3, perm, unperm, bdr, bdi, ccat, slabs(a1), slabs(a2), d.reshape(1, W_BR), w_glu,
      b_glu.reshape(1, W_BR))
    return y, hT.transpose(1, 0, 2).reshape(2 * nb, S5_W)


def _s5_sample_kernel(u_ref, g_ref, h0r_ref, h0i_ref, bdr_ref, bdi_ref, cdr_ref, cdi_ref,
                      ar_ref, ai_ref, d_ref, wglu_ref, bglu_ref, y_ref, hr_ref, hi_ref):
    u = u_ref[...]
    ub = u.astype(BF16)
    ar, ai = ar_ref[...], ai_ref[...]
    h0r, h0i = h0r_ref[...], h0i_ref[...]
    hr = jnp.dot(ub, bdr_ref[...], preferred_element_type=F32) + (ar * h0r - ai * h0i)
    hi = jnp.dot(ub, bdi_ref[...], preferred_element_type=F32) + (ar * h0i + ai * h0r)
    hr_ref[...] = hr
    hi_ref[...] = hi
    y = _bdot(hr, cdr_ref[...]) - _bdot(hi, cdi_ref[...])
    y_ref[...] = _s5_tail(y, u, g_ref[...], d_ref, wglu_ref, bglu_ref).astype(y_ref.dtype)


def _s5_sample(z, h0_re, h0_im, dense, abar_re, abar_im, d, w_glu, b_glu):
    nb = z.shape[0]
    bdr, bdi, cdr, cdi = dense[:4]
    full = lambda shape: pl.BlockSpec(shape, lambda i: (0,) * len(shape))
    st = jax.ShapeDtypeStruct((nb, S5_W), F32)
    return pl.pallas_call(
        _s5_sample_kernel,
        grid=(1,),
        in_specs=[
            pl.BlockSpec((nb, W_BR), lambda i: (0, COL_A_U // W_BR)),
            pl.BlockSpec((nb, W_BR), lambda i: (0, COL_A_G // W_BR)),
            full((nb, S5_W)), full((nb, S5_W)),
            full((W_BR, S5_W)), full((W_BR, S5_W)), full((S5_W, W_BR)), full((S5_W, W_BR)),
            full((1, S5_W)), full((1, S5_W)),
            full((1, W_BR)), full((W_BR, W_BR)), full((1, W_BR)),
        ],
        out_specs=[full((nb, W_BR)), full((nb, S5_W)), full((nb, S5_W))],
        out_shape=(jax.ShapeDtypeStruct((nb, W_BR), BF16), st, st),
        compiler_params=_params(("arbitrary",)),
        name="s5_sample",
    )(z, z, h0_re, h0_im, bdr, bdi, cdr, cdi, abar_re.reshape(1, S5_W), abar_im.reshape(1, S5_W),
      d.reshape(1, W_BR), w_glu, b_glu.reshape(1, W_BR))


def _layernorm(v, g_ref, b_ref):
    mu = jnp.mean(v, axis=-1, keepdims=True)
    vc = v - mu
    var = jnp.mean(vc * vc, axis=-1, keepdims=True)
    return vc * lax.rsqrt(var + LN_EPS) * g_ref[...] + b_ref[...]


def _sgu_prompt_kernel(u_ref, v_ref, g_ref, lng_ref, lnb_ref, ws_ref, bs_ref, y_ref, *, n_chunks):
    cw = W_BR // HG_B
    vn = _layernorm(_gelu(v_ref[...]), lng_ref, lnb_ref)
    row = lax.broadcasted_iota(jnp.int32, (CHUNK, CHUNK), 0)
    col = lax.broadcasted_iota(jnp.int32, (CHUNK, CHUNK), 1)
    ws = [jnp.where(col <= row, ws_ref[h], 0.0).astype(BF16) for h in range(HG_B)]
    bs = bs_ref[...]
    for c in range(n_chunks):
        rows = slice(c * CHUNK, (c + 1) * CHUNK)
        vb = vn[rows, :].astype(BF16)
        mix = jnp.concatenate(
            [jnp.dot(ws[h], vb[:, h * cw:(h + 1) * cw], preferred_element_type=F32) for h in range(HG_B)],
            axis=1) + bs
        out = _gelu(u_ref[rows, :]) * mix * _silu(g_ref[rows, :])
        y_ref[rows, :] = out.astype(y_ref.dtype)


def _sgu_prompt(z, ln_g, ln_b, w_s, b_s):
    n = z.shape[0]
    tm = min(512, n)
    full = lambda shape: pl.BlockSpec(shape, lambda i: (0,) * len(shape))
    bs_full = jnp.repeat(b_s.T, W_BR // HG_B, axis=1)
    kern = functools.partial(_sgu_prompt_kernel, n_chunks=tm // CHUNK)
    col = lambda c: pl.BlockSpec((tm, W_BR), lambda i: (i, c // W_BR))
    return pl.pallas_call(
        kern,
        grid=(n // tm,),
        in_specs=[col(COL_B_U), col(COL_B_V), col(COL_B_G),
                  full((1, W_BR)), full((1, W_BR)), full((HG_B, CHUNK, CHUNK)), full((CHUNK, W_BR))],
        out_specs=pl.BlockSpec((tm, W_BR), lambda i: (i, 0)),
        out_shape=jax.ShapeDtypeStruct((n, W_BR), BF16),
        compiler_params=_params(("arbitrary",)),
        name="sgu_prompt",
    )(z, z, z, ln_g.reshape(1, W_BR), ln_b.reshape(1, W_BR), w_s, bs_full)


def _sgu_sample_kernel(u_ref, v_ref, g_ref, lng_ref, lnb_ref, w00_ref, b0_ref, y_ref, vn_ref):
    vn = _layernorm(_gelu(v_ref[...]), lng_ref, lnb_ref)
    vn_ref[...] = vn
    mix = vn * w00_ref[...] + b0_ref[...]
    y_ref[...] = (_gelu(u_ref[...]) * mix * _silu(g_ref[...])).astype(y_ref.dtype)


def _sgu_sample(z, ln_g, ln_b, w_s, b_s):
    nb = z.shape[0]
    cw = W_BR // HG_B
    full = lambda shape: pl.BlockSpec(shape, lambda i: (0,) * len(shape))
    col = lambda c: pl.BlockSpec((nb, W_BR), lambda i: (0, c // W_BR))
    w00 = jnp.repeat(w_s[:, 0, 0], cw).reshape(1, W_BR)
    b0 = jnp.repeat(b_s[:, 0], cw).reshape(1, W_BR)
    return pl.pallas_call(
        _sgu_sample_kernel,
        grid=(1,),
        in_specs=[col(COL_B_U), col(COL_B_V), col(COL_B_G),
                  full((1, W_BR)), full((1, W_BR)), full((1, W_BR)), full((1, W_BR))],
        out_specs=[full((nb, W_BR)), full((nb, W_BR))],
        out_shape=(jax.ShapeDtypeStruct((nb, W_BR), BF16), jax.ShapeDtypeStruct((nb, W_BR), F32)),
        compiler_params=_params(("arbitrary",)),
        name="sgu_sample",
    )(z, z, z, ln_g.reshape(1, W_BR), ln_b.reshape(1, W_BR), w00, b0)


def _suffix_matrix():
    j = np.arange(LANE)[:, None]
    s = np.arange(LANE)[None, :]
    return jnp.asarray(np.concatenate([(j >= s), np.ones((LANE, LANE), bool)], axis=1), dtype=BF16)


LOG2E = 1.0 / math.log(2.0)


def _softplus2(z):
    return jnp.maximum(z, 0.0) + jnp.log2(1.0 + jnp.exp2(-jnp.abs(z)))


SB_HEADS_PER_STEP = 4


def _sb_prompt_kernel(it_ref, jt_ref, bias_ref, q_ref, k_ref, v_ref, g_ref, m_ref, o_ref, *scr, tq, tk):
    nh = SB_HEADS_PER_STEP
    hb = pl.program_id(1)
    p = pl.program_id(2)
    i = it_ref[p]
    j = jt_ref[p]
    nsub = tk // LANE
    accs, cars = scr[:nh], scr[nh:]

    @pl.when(j == 0)
    def _():
        for r in scr:
            r[...] = jnp.zeros_like(r)

    head_of_lane = lax.broadcasted_iota(jnp.int32, (tq, nh * HD_C), 1) // HD_C

    def process(masked):
        q = q_ref[...] * (HD_C ** -0.5 * LOG2E)
        qh = [jnp.where(head_of_lane == h, q, 0.0).astype(BF16) for h in range(nh)]
        m = m_ref[...]
        kb = k_ref[...].astype(BF16)
        vb = v_ref[...].astype(BF16)
        n_part = 2 if tq % (2 * SUBLANE) == 0 else 1
        pr = tq // n_part
        chunks = [(h, r) for h in range(nh) for r in range(n_part)]
        rows = lambda r: slice(r * pr, (r + 1) * pr)
        if masked:
            valid = [(lax.broadcasted_iota(jnp.int32, (pr, tk), 1)
                      < lax.broadcasted_iota(jnp.int32, (pr, tk), 0) + r * pr) for r in range(n_part)]
        zs = [lax.dot_general(qh[h][rows(r), :], kb, (((1,), (1,)), ((), ())), preferred_element_type=F32)
              + bias_ref[nh * hb + h] * LOG2E for (h, r) in chunks]
        sps = []
        for n, (h, r) in enumerate(chunks):
            sp = _softplus2(zs[n])
            sps.append((jnp.where(valid[r], sp, 0.0) if masked else sp).astype(BF16))
        rss = [[jnp.dot(sps[n][:, sub * LANE:(sub + 1) * LANE], m, preferred_element_type=F32)
                for sub in range(nsub)] for n in range(len(chunks))]
        ebs = []
        for n, (h, r) in enumerate(chunks):
            c = cars[h][rows(r), :]
            es = [None] * nsub
            for sub in reversed(range(nsub)):
                es[sub] = jnp.exp2(zs[n][:, sub * LANE:(sub + 1) * LANE] - rss[n][sub][:, :LANE] - c)
                c = c + rss[n][sub][:, LANE:]
            cars[h][rows(r), :] = c
            e = jnp.concatenate(es, axis=1)
            ebs.append((jnp.where(valid[r], e, 0.0) if masked else e).astype(BF16))
        for n, (h, r) in enumerate(chunks):
            accs[h][rows(r), :] += jnp.dot(ebs[n], vb, preferred_element_type=F32)

    @pl.when(j == 0)
    def _():
        process(True)

    @pl.when(j > 0)
    def _():
        process(False)

    @pl.when(j == i)
    def _():
        y = accs[0][...]
        for h in range(1, nh):
            y = jnp.where(head_of_lane == h, accs[h][...], y)
        o_ref[...] = (y * _silu(g_ref[...])).astype(o_ref.dtype)


def _sb_prompt(z, sb_bias, nb, t_len):
    tq = tk = min(512, t_len)
    nq = t_len // tq
    nh = SB_HEADS_PER_STEP
    bw = nh * HD_C
    pairs = [(i, j) for i in range(nq) for j in range(i + 1)]
    it = jnp.asarray([p[0] for p in pairs], jnp.int32)
    jt = jnp.asarray([p[1] for p in pairs], jnp.int32)
    kern = functools.partial(_sb_prompt_kernel, tq=tq, tk=tk)
    qspec = lambda c: pl.BlockSpec((tq, bw), lambda b, h, p, it, jt: (b * nq + it[p], c // bw + h))
    kspec = lambda c: pl.BlockSpec((tk, bw), lambda b, h, p, it, jt: (b * nq + it[p] - jt[p], c // bw + h))
    grid_spec = pltpu.PrefetchScalarGridSpec(
        num_scalar_prefetch=2,
        grid=(nb, H_C // nh, len(pairs)),
        in_specs=[
            pl.BlockSpec(memory_space=pltpu.SMEM),
            qspec(COL_C_Q), kspec(COL_C_K), kspec(COL_C_V), qspec(COL_C_G),
            pl.BlockSpec((LANE, 2 * LANE), lambda b, h, p, it, jt: (0, 0)),
        ],
        out_specs=pl.BlockSpec((tq, bw), lambda b, h, p, it, jt: (b * nq + it[p], h)),
        scratch_shapes=[pltpu.VMEM((tq, bw), F32)] * nh + [pltpu.VMEM((tq, LANE), F32)] * nh,
    )
    return pl.pallas_call(
        kern,
        grid_spec=grid_spec,
        out_shape=jax.ShapeDtypeStruct((nb * t_len, W_BR), BF16),
        compiler_params=_params(("arbitrary",) * 3),
        name="sb_prompt",
    )(it, jt, sb_bias, z, z, z, z, _suffix_matrix())


PAGES_PER_STEP = 16


def _sb_sample_kernel(pt_ref, qb_ref, g_ref, bias_ref, m_ref, *rest):
    k_refs = rest[:PAGES_PER_STEP]
    v_refs = rest[PAGES_PER_STEP:2 * PAGES_PER_STEP]
    o_ref, acc, car = rest[2 * PAGES_PER_STEP:]
    jj = pl.program_id(1)

    @pl.when(jj == 0)
    def _():
        acc[...] = jnp.zeros_like(acc)
        car[...] = jnp.zeros_like(car)

    qb = qb_ref[...] * (HD_C ** -0.5 * LOG2E)
    bias = bias_ref[...] * LOG2E
    m = m_ref[...]
    for r in reversed(range(PAGES_PER_STEP)):
        prod = k_refs[r][...] * qb
        z = jnp.concatenate([jnp.sum(prod[h * HD_C:(h + 1) * HD_C, :], axis=0, keepdims=True)
                             for h in range(H_C)], axis=0) + bias
        rr = _split_dot(_softplus2(z), m, 2)
        e = jnp.exp2(z - rr[:, :LANE] - car[...])
        car[...] += rr[:, LANE:]
        eb = jnp.concatenate([jnp.broadcast_to(e[h:h + 1, :], (HD_C, LANE)) for h in range(H_C)], axis=0)
        acc[...] += v_refs[r][...] * eb

    @pl.when(jj == pl.num_programs(1) - 1)
    def _():
        o_ref[...] = jnp.sum(acc[...], axis=1, keepdims=True) * _silu(g_ref[...])


def _sb_sample(z, cache_k, cache_v, page_table, sb_bias, base):
    nb = z.shape[0]
    n_pages = page_table.shape[1]
    page = cache_k.shape[2]
    assert page == LANE and n_pages % PAGES_PER_STEP == 0
    n_steps = n_pages // PAGES_PER_STEP
    bias = jnp.broadcast_to(sb_bias[:, None], (H_C, page))
    qb = jnp.broadcast_to(z[:, COL_C_Q:COL_C_Q + W_BR, None], (nb, W_BR, page))
    g3 = z[:, COL_C_G:COL_C_G + W_BR].reshape(nb, W_BR, 1)

    def page_spec(r):
        def imap(b, jj, pt):
            return (base + pt[b, (n_steps - 1 - jj) * PAGES_PER_STEP + r], 0, 0)
        return pl.BlockSpec((None, W_BR, page), imap)

    const = lambda shape: pl.BlockSpec(shape, lambda b, jj, pt: (0,) * len(shape))
    col_spec = pl.BlockSpec((None, W_BR, 1), lambda b, jj, pt: (b, 0, 0))
    grid_spec = pltpu.PrefetchScalarGridSpec(
        num_scalar_prefetch=1,
        grid=(nb, n_steps),
        in_specs=[pl.BlockSpec((None, W_BR, page), lambda b, jj, pt: (b, 0, 0)), col_spec,
                  const((H_C, page)), const((LANE, 2 * LANE))]
                 + [page_spec(r) for r in range(PAGES_PER_STEP)]
                 + [page_spec(r) for r in range(PAGES_PER_STEP)],
        out_specs=col_spec,
        scratch_shapes=[pltpu.VMEM((W_BR, page), F32), pltpu.VMEM((H_C, LANE), F32)],
    )
    out = pl.pallas_call(
        _sb_sample_kernel,
        grid_spec=grid_spec,
        out_shape=jax.ShapeDtypeStruct((nb, W_BR, 1), F32),
        compiler_params=_params(("arbitrary", "arbitrary")),
        name="sb_sample",
    )(page_table, qb, g3, bias, _suffix_matrix(),
      *([cache_k] * PAGES_PER_STEP), *([cache_v] * PAGES_PER_STEP))
    return out.reshape(nb, W_BR)


def _head_ones(width):
    idx = np.arange(width) // N_D
    return jnp.asarray(idx[:, None] == idx[None, :], dtype=BF16)


def _rwkv_prep_kernel(r_ref, k_ref, v_ref, wa_ref, pr_ref, pk_ref, pv_ref, pwa_ref,
                      mur_ref, muk_ref, muv_ref, muwa_ref, w0_ref, w2_ref, a0_ref, a2_ref,
                      kk_ref_, ka_ref, rk_ref, eh_ref,
                      ro_ref, ko_ref, vo_ref, dec_ref, kko_ref, bo_ref, bon_ref, *, roll_rows, first_zero):
    def shifted(x_ref, p_ref, mu_ref):
        x = x_ref[...]
        if roll_rows:
            row = lax.broadcasted_iota(jnp.int32, x.shape, 0)
            last = p_ref[SUBLANE - 1:SUBLANE, :]
            if first_zero:
                last = jnp.where(pl.program_id(1) == 0, 0.0, last)
            prev = jnp.where(row == 0, last, pltpu.roll(x, 1, 0))
        else:
            prev = p_ref[...]
        return x + (prev - x) * mu_ref[...]

    r = shifted(r_ref, pr_ref, mur_ref)
    k = shifted(k_ref, pk_ref, muk_ref)
    v = shifted(v_ref, pv_ref, muv_ref)
    wa = shifted(wa_ref, pwa_ref, muwa_ref)
    eh = eh_ref[...]
    w = -_softplus(-(w0_ref[...] + _bdot(jnp.tanh(wa), w2_ref[...]))) - 0.5
    a = jax.nn.sigmoid(a0_ref[...] + _bdot(wa, a2_ref[...]))
    kk = k * kk_ref_[...]
    kk = kk * lax.rsqrt(_split_dot(kk * kk, eh, 3) + 1e-12)
    k2 = k * (1.0 + (a - 1.0) * ka_ref[...])
    ro_ref[...] = r
    ko_ref[...] = k2
    vo_ref[...] = v
    dec_ref[...] = jnp.exp(-jnp.exp(w))
    kko_ref[...] = kk
    bo_ref[...] = kk * a
    bon_ref[...] = _split_dot(r * k2 * rk_ref[...], eh, 3) * v


def _rwkv_prep(z, prev, nb, t_len, mu, w0, w2, a0, a2, k_k, k_a, r_k):
    n = z.shape[0]
    tm = min(512, t_len) if prev is None else n
    bps = max(t_len // tm, 1)
    wa_w = LR_W + LR_A
    row = lambda v: v.reshape(1, -1)
    col = lambda c, w: pl.BlockSpec((tm, w), lambda b, i: (b * bps + i, c // w))
    if prev is None:
        tsub = tm // SUBLANE
        pcol = lambda c, w: pl.BlockSpec(
            (SUBLANE, w), lambda b, i: (jnp.maximum((b * bps + i) * tsub - 1, 0), c // w))
        prev_specs = [pcol(COL_D_R, W_BR), pcol(COL_D_K, W_BR), pcol(COL_D_V, W_BR), pcol(COL_D_WA, wa_w)]
        prev_args = [z, z, z, z]
        grid = (nb, bps)
    else:
        pfull = lambda w: pl.BlockSpec((tm, w), lambda b, i: (0, 0))
        prev_specs = [pfull(W_BR), pfull(W_BR), pfull(W_BR), pfull(wa_w)]
        prev_args = [prev[:, 0:W_BR], prev[:, W_BR:2 * W_BR], prev[:, 2 * W_BR:3 * W_BR], prev[:, 3 * W_BR:]]
        grid = (1, 1)
    full = lambda shape: pl.BlockSpec(shape, lambda b, i: (0,) * len(shape))
    w2p = jnp.concatenate([w2, jnp.zeros((LR_A, W_BR), F32)], axis=0).astype(BF16)
    a2p = jnp.concatenate([jnp.zeros((LR_W, W_BR), F32), a2], axis=0).astype(BF16)
    kern = functools.partial(_rwkv_prep_kernel, roll_rows=prev is None, first_zero=prev is None)
    out = jax.ShapeDtypeStruct((n, W_BR), F32)
    ospec = pl.BlockSpec((tm, W_BR), lambda b, i: (b * bps + i, 0))
    return pl.pallas_call(
        kern,
        grid=grid,
        in_specs=[col(COL_D_R, W_BR), col(COL_D_K, W_BR), col(COL_D_V, W_BR), col(COL_D_WA, wa_w)]
                 + prev_specs
                 + [full((1, W_BR))] * 3 + [full((1, wa_w))]
                 + [full((1, W_BR)), full((wa_w, W_BR)), full((1, W_BR)), full((wa_w, W_BR))]
                 + [full((1, W_BR))] * 3 + [full((W_BR, W_BR))],
        out_specs=[ospec] * 7,
        out_shape=(out,) * 7,
        compiler_params=_params(("arbitrary", "arbitrary")),
        name="rwkv_prep",
    )(z, z, z, z, *prev_args,
      row(mu[0:W_BR]), row(mu[W_BR:2 * W_BR]), row(mu[2 * W_BR:3 * W_BR]), row(mu[3 * W_BR:]),
      row(w0), w2p, row(a0), a2p, row(k_k), row(k_a), row(r_k), _head_ones(W_BR))


def _rwkv_scan_kernel(r_ref, k_ref, v_ref, dec_ref, kk_ref, b_ref, bon_ref, g_ref, s0_ref,
                      lng_ref, lnb_ref, e4_ref, eh_ref, y_ref, sT_ref, s_scr, y_scr,
                      *, nb, lc, state_per_row):
    n_hp = H_D // 2
    c = pl.program_id(0)

    @pl.when(c == 0)
    def _():
        s_scr[...] = s0_ref[...]

    e4 = e4_ref[...]
    sub = lax.broadcasted_iota(jnp.int32, (N_D, LANE), 0)
    lane = lax.broadcasted_iota(jnp.int32, (N_D, LANE), 1)
    diag = (lane % N_D) == sub
    pack = 2 * SUBLANE
    diag_b = diag.astype(F32).astype(BF16).reshape(N_D // pack, pack, LANE)
    combos = [(b, hp) for b in range(nb) for hp in range(n_hp)]
    nc = len(combos)
    assert nc % 2 == 0

    def pairs(slabs):
        return [jnp.concatenate([slabs[q], slabs[q + 1]], axis=1) for q in range(0, len(slabs), 2)]

    def block(out, first_row, n):
        r = first_row + (n // 2) * N_D
        return out[r:r + N_D, (n % 2) * LANE:(n % 2 + 1) * LANE]

    def group(gi, carry):
        r0 = pl.multiple_of(gi * SUBLANE, SUBLANE)
        rows8 = pl.ds(r0, SUBLANE)
        load = lambda ref: [ref[b, rows8, :] for b in range(nb)]
        rr, kx, vx, dx, kkx, bx = (load(x) for x in (r_ref, k_ref, v_ref, dec_ref, kk_ref, b_ref))
        row = lambda blocks, b, hp, tt: blocks[b][tt:tt + 1, hp * LANE:(hp + 1) * LANE]
        state = lambda tt: [s_scr[((r0 + tt) if state_per_row else b) * n_hp + hp] for (b, hp) in combos]
        kk_rows = lambda s, tt: pairs([(s[c] * (-row(kkx, b, hp, tt))).astype(BF16)
                                       for c, (b, hp) in enumerate(combos)])

        def vdiag(tt, b, hp):
            piece = jnp.broadcast_to(row(vx, b, hp, tt), (pack, LANE)).astype(BF16)
            return (diag_b * piece[None]).reshape(N_D, LANE)

        s = state(0)
        out = jnp.dot(jnp.concatenate(
            kk_rows(s, 0) + pairs([vdiag(tt, b, hp) for tt in range(SUBLANE) for (b, hp) in combos]), axis=0),
            e4, preferred_element_type=F32)
        sa = [block(out, 0, c) for c in range(nc)]
        vb = [block(out, nc // 2 * N_D, n) for n in range(SUBLANE * nc)]

        y_rows = [[[None] * n_hp for _ in range(SUBLANE)] for _ in range(nb)]
        for tt in range(SUBLANE):
            last = tt == SUBLANE - 1
            new = []
            for c, (b, hp) in enumerate(combos):
                sn = (s[c] * row(dx, b, hp, tt) + sa[c] * row(bx, b, hp, tt)
                      + vb[tt * nc + c] * row(kx, b, hp, tt))
                if state_per_row or last:
                    s_scr[((r0 + tt) if state_per_row else b) * n_hp + hp] = sn
                new.append(sn)
            y_lhs = pairs([(new[c] * row(rr, b, hp, tt)).astype(BF16) for c, (b, hp) in enumerate(combos)])
            if last:
                lhs, y0 = y_lhs, 0
            else:
                s = state(tt + 1) if state_per_row else new
                lhs, y0 = kk_rows(s, tt + 1) + y_lhs, nc // 2 * N_D
            out = jnp.dot(jnp.concatenate(lhs, axis=0), e4, preferred_element_type=F32)
            if not last:
                sa = [block(out, 0, c) for c in range(nc)]
            for c, (b, hp) in enumerate(combos):
                y_rows[b][tt][hp] = jnp.sum(jnp.where(diag, block(out, y0, c), 0.0), axis=0, keepdims=True)
        for b in range(nb):
            y_scr[b, rows8, :] = jnp.concatenate(
                [jnp.concatenate(y_rows[b][tt], axis=1) for tt in range(SUBLANE)], axis=0)
        return carry

    lax.fori_loop(0, lc // SUBLANE, group, 0)

    sT_ref[...] = s_scr[...]
    eh = eh_ref[...]
    for b in range(nb):
        y = y_scr[b]
        ym = _split_dot(y, eh, 2) * (1.0 / N_D)
        yc = y - ym
        yv = _split_dot(yc * yc, eh, 2) * (1.0 / N_D)
        out = yc * lax.rsqrt(yv + GN_EPS) * lng_ref[...] + lnb_ref[...] + bon_ref[b]
        y_ref[b] = (out * _silu(g_ref[b])).astype(y_ref.dtype)


def _rwkv_scan(prep, z3, s0, lnx_g, lnx_b, state_per_row):
    nb, t_len, _ = z3.shape
    lc = t_len if state_per_row else min(128, t_len)
    p3 = [p.reshape(nb, t_len, W_BR) for p in prep]
    blk = pl.BlockSpec((nb, lc, W_BR), lambda c: (0, c, 0))
    full = lambda shape: pl.BlockSpec(shape, lambda c: (0,) * len(shape))
    kern = functools.partial(_rwkv_scan_kernel, nb=nb, lc=lc, state_per_row=state_per_row)
    n_st = s0.shape[0]
    return pl.pallas_call(
        kern,
        grid=(t_len // lc,),
        in_specs=[blk] * 7
                 + [pl.BlockSpec((nb, lc, W_BR), lambda c: (0, c, COL_D_G // W_BR)),
                    full((n_st, N_D, LANE)), full((1, W_BR)), full((1, W_BR)),
                    full((2 * LANE, 2 * LANE)), full((W_BR, W_BR))],
        out_specs=[blk, full((n_st, N_D, LANE))],
        out_shape=(jax.ShapeDtypeStruct((nb, t_len, W_BR), BF16),
                   jax.ShapeDtypeStruct((n_st, N_D, LANE), F32)),
        scratch_shapes=[pltpu.VMEM((n_st, N_D, LANE), F32), pltpu.VMEM((nb, lc, W_BR), F32)],
        compiler_params=_params(("arbitrary",)),
        name="rwkv_scan",
    )(p3[0], p3[1], p3[2], p3[3], p3[4], p3[5], p3[6], z3, s0,
      lnx_g.reshape(1, W_BR), lnx_b.reshape(1, W_BR), _head_ones(2 * LANE), _head_ones(W_BR))


def _wkv_to_pairs(s):
    nb = s.shape[0]
    return s.reshape(nb, H_D // 2, 2, N_D, N_D).transpose(0, 1, 3, 2, 4).reshape(nb * H_D // 2, N_D, LANE)


def _wkv_from_pairs(s, nb):
    return s.reshape(nb, H_D // 2, N_D, 2, N_D).transpose(0, 1, 3, 2, 4).reshape(nb, H_D, N_D, N_D)


def _merge_kernel(x_ref, ya_ref, yb_ref, yc_ref, yd_ref, mga_ref, mgb_ref, mgc_ref, mgd_ref,
                  gt_ref, wb_ref, wo_ref, fg_ref, *out_refs, final):
    merged = None
    branches = ((ya_ref, mga_ref), (yb_ref, mgb_ref), (yc_ref, mgc_ref), (yd_ref, mgd_ref))
    for n, (y_ref, mg_ref) in enumerate(branches):
        proj = jnp.dot(y_ref[...].astype(BF16), wb_ref[n], preferred_element_type=F32)
        term = jax.nn.sigmoid(mg_ref[...]) * proj
        merged = term if merged is None else merged + term
    x_new = x_ref[...] + gt_ref[...] * _bdot(merged, wo_ref[...])
    out_refs[0][...] = x_new
    if final:
        ms = jnp.mean(x_new * x_new, axis=-1, keepdims=True)
        out_refs[1][...] = x_new * lax.rsqrt(ms + RMS_EPS) * fg_ref[...]


def _merge(x, ys, z, mod, w_branch, w_out, layer, final_g, rows_per_group, final):
    n = x.shape[0]
    tm = min(256, n)
    r = mod.shape[1]
    bpg = rows_per_group // tm
    full = lambda shape: pl.BlockSpec(shape, lambda i: (0,) * len(shape))
    yspec = pl.BlockSpec((tm, W_BR), lambda i: (i, 0))
    xspec = pl.BlockSpec((tm, D_MODEL), lambda i: (i, 0))
    out_sds = jax.ShapeDtypeStruct((n, D_MODEL), F32)
    outs = pl.pallas_call(
        functools.partial(_merge_kernel, final=final),
        grid=(n // tm,),
        in_specs=[xspec, yspec, yspec, yspec, yspec]
                 + [pl.BlockSpec((tm, D_MODEL), lambda i, n=n: (i, COL_M_G // D_MODEL + n)) for n in range(N_BR)]
                 + [pl.BlockSpec((None, r, D_MODEL), lambda i: (i // bpg, 0, 2)),
                  pl.BlockSpec((None, N_BR, W_BR, D_MODEL), lambda i: (layer, 0, 0, 0)),
                  pl.BlockSpec((None, D_MODEL, D_MODEL), lambda i: (layer, 0, 0)), full((1, D_MODEL))],
        out_specs=[xspec, xspec] if final else [xspec],
        out_shape=(out_sds, out_sds) if final else (out_sds,),
        compiler_params=_params(("arbitrary",)),
        name="merge",
    )(x, *ys, z, z, z, z, mod, w_branch, w_out, final_g.reshape(1, D_MODEL))
    return outs


def _pad_w_in(w_in):
    depth = w_in.shape[0]
    pad = jnp.zeros((depth, D_MODEL, COL_D_G - IN_W_SRC_D_G), w_in.dtype)
    return jnp.concatenate([w_in[:, :, :IN_W_SRC_D_G], pad, w_in[:, :, IN_W_SRC_D_G:]], axis=2).astype(BF16)


def kernel(x_prompt, x_sample, cache_k, cache_v, state_ssm_re, state_ssm_im, state_wkv, state_shift, page_table, c_prompt, c_sample, norm_g, w_ada, b_ada, w_in, ssm_a_re, ssm_a_im, ssm_log_dt, ssm_b_re, ssm_b_im, ssm_c_re, ssm_c_im, ssm_d, ssm_w_glu, ssm_b_glu, sgu_ln_g, sgu_ln_b, sgu_w, sgu_b, sb_bias, rwkv_mu, rwkv_w0, rwkv_w2, rwkv_a0, rwkv_a2, rwkv_k_k, rwkv_k_a, rwkv_r_k, rwkv_lnx_g, rwkv_lnx_b, w_branch, w_out, final_norm_g):
    bp, t_len, _ = x_prompt.shape
    db = x_sample.shape[0]
    depth = w_in.shape[0]
    n_pool, page = cache_k.shape[1], cache_k.shape[2]
    np_rows = bp * t_len

    n_c = bp + db
    c_rows = -(-n_c // SUBLANE) * SUBLANE
    c_all = jnp.concatenate([c_prompt, c_sample, jnp.zeros((c_rows - n_c, D_MODEL), F32)], axis=0)
    mod = _ada_mod(c_all, w_ada, b_ada)
    w_in_p = _pad_w_in(w_in)
    w_branch_b = w_branch.astype(BF16)
    w_out_b = w_out.astype(BF16)
    w_glu_b = ssm_w_glu.astype(BF16)
    ck = cache_k.transpose(0, 1, 3, 4, 2).reshape(depth * n_pool, W_BR, page)
    cv = cache_v.transpose(0, 1, 3, 4, 2).reshape(depth * n_pool, W_BR, page)

    xp = x_prompt.reshape(np_rows, D_MODEL)
    xs = x_sample.reshape(db, D_MODEL)
    zero_wkv = jnp.zeros((bp * H_D // 2, N_D, LANE), F32)
    outs_p, outs_s = [], []
    yp = ys_out = None
    for l in range(depth):
        final = l == depth - 1
        mod_p = mod[l, :bp].reshape(bp, 1, 3 * D_MODEL)
        mod_s = mod[l, bp:bp + db].reshape(1, db, 3 * D_MODEL)
        abr, abi, bbr, bbi = _s5_discretise(ssm_a_re[l], ssm_a_im[l], ssm_log_dt[l], ssm_b_re[l], ssm_b_im[l])
        dense = _s5_dense_weights(abr, abi, bbr, bbi, ssm_c_re[l], ssm_c_im[l], bp)
        rw = (rwkv_mu[l], rwkv_w0[l], rwkv_w2[l], rwkv_a0[l], rwkv_a2[l], rwkv_k_k[l], rwkv_k_a[l],
              rwkv_r_k[l].reshape(W_BR))

        z = _in_proj(xp, mod_p, norm_g[l], w_in_p, l, t_len)
        z3 = z.reshape(bp, t_len, IN_W_PAD)
        ya, hT = _s5_prompt(z3, dense, ssm_d[l], w_glu_b[l], ssm_b_glu[l])
        yb = _sgu_prompt(z, sgu_ln_g[l], sgu_ln_b[l], sgu_w[l], sgu_b[l])
        yc = _sb_prompt(z, sb_bias[l], bp, t_len)
        prep = _rwkv_prep(z, None, bp, t_len, *rw)
        yd, sT = _rwkv_scan(prep, z3, zero_wkv, rwkv_lnx_g[l], rwkv_lnx_b[l], False)
        res = _merge(xp, (ya.reshape(np_rows, W_BR), yb, yc, yd.reshape(np_rows, W_BR)), z, mod_p,
                     w_branch_b, w_out_b, l, final_norm_g, t_len, final)
        xp = res[0]
        if final:
            yp = res[1]
        outs_p.append((
            z3[:, :, COL_C_K:COL_C_K + W_BR].reshape(bp, t_len, H_C, HD_C),
            z3[:, :, COL_C_V:COL_C_V + W_BR].reshape(bp, t_len, H_C, HD_C),
            hT[:bp].reshape(bp, G_A, P_A), hT[bp:].reshape(bp, G_A, P_A),
            _wkv_from_pairs(sT, bp),
            z3[:, -1, COL_D_R:COL_D_R + SHIFT_W],
        ))

        zs = _in_proj(xs, mod_s, norm_g[l], w_in_p, l, db)
        sa, hr, hi = _s5_sample(zs, state_ssm_re[l].reshape(db, S5_W), state_ssm_im[l].reshape(db, S5_W),
                                dense, abr, abi, ssm_d[l], w_glu_b[l], ssm_b_glu[l])
        sb, v_rows = _sgu_sample(zs, sgu_ln_g[l], sgu_ln_b[l], sgu_w[l], sgu_b[l])
        sc = _sb_sample(zs, ck, cv, page_table, sb_bias[l], l * n_pool)
        preps = _rwkv_prep(zs, state_shift[l], db, 1, *rw)
        sd, sTs = _rwkv_scan(preps, zs.reshape(1, db, IN_W_PAD), _wkv_to_pairs(state_wkv[l]),
                             rwkv_lnx_g[l], rwkv_lnx_b[l], True)
        res = _merge(xs, (sa, sb, sc, sd.reshape(db, W_BR)), zs, mod_s,
                     w_branch_b, w_out_b, l, final_norm_g, db, final)
        xs = res[0]
        if final:
            ys_out = res[1]
        outs_s.append((
            zs[:, COL_C_K:COL_C_K + W_BR].reshape(db, 1, H_C, HD_C),
            zs[:, COL_C_V:COL_C_V + W_BR].reshape(db, 1, H_C, HD_C),
            hr.reshape(db, G_A, P_A), hi.reshape(db, G_A, P_A),
            _wkv_from_pairs(sTs, db),
            zs[:, COL_D_R:COL_D_R + SHIFT_W],
            v_rows.reshape(db, 1, W_BR),
        ))

    stk = lambda outs, i: jnp.stack([o[i] for o in outs])
    return (yp.reshape(bp, t_len, D_MODEL), ys_out.reshape(db, 1, D_MODEL),
            stk(outs_p, 0), stk(outs_p, 1), stk(outs_s, 0), stk(outs_s, 1),
            stk(outs_p, 2), stk(outs_p, 3), stk(outs_s, 2), stk(outs_s, 3),
            stk(outs_p, 4), stk(outs_s, 4), stk(outs_p, 5), stk(outs_s, 5), stk(outs_s, 6))
```
